```python
import jax, jax.numpy as jnp
from jax import lax
import numpy as np

D_MODEL = 2048
BATCH = 2
SEQ = 8192
DEPTH = 4

N_Q_HEADS = 16
N_KV_HEADS = 4
HEAD_DIM = 64
Q_PER_KV = N_Q_HEADS // N_KV_HEADS
ATTN_WIDTH = N_Q_HEADS * HEAD_DIM
KV_WIDTH = N_KV_HEADS * HEAD_DIM
WINDOW = 128
ROPE_THETA = 500000.0
ROPE_DIM = HEAD_DIM // 4
SGU_WIDTH = D_MODEL // 2
SGU_GROUPS = 8
SGU_GROUP_DIM = SGU_WIDTH // SGU_GROUPS
CHUNK = 128
D_FF = -(-8 * D_MODEL // (3 * 256)) * 256
OFF_Q = 0
OFF_K = OFF_Q + ATTN_WIDTH
OFF_V = OFF_K + KV_WIDTH
OFF_Z = OFF_V + KV_WIDTH
OFF_G = OFF_Z + 2 * SGU_WIDTH
IN_WIDTH = OFF_G + 2 * D_MODEL
EPS = 1e-5
NEG = -1e30

kernel_name = "hybrid_swa_sink_gmlp_gated_block"


def rmsnorm(x, g):
    xf = x.astype(jnp.float32)
    y = xf * lax.rsqrt(jnp.mean(xf * xf, axis=-1, keepdims=True) + EPS) * g.astype(jnp.float32)
    return y.astype(x.dtype)


def partial_rope(t, cos, sin):
    half = ROPE_DIM // 2
    t1 = t[..., :half].astype(jnp.float32)
    t2 = t[..., half:ROPE_DIM].astype(jnp.float32)
    rot = jnp.concatenate([t1 * cos - t2 * sin, t2 * cos + t1 * sin], axis=-1).astype(t.dtype)
    return jnp.concatenate([rot, t[..., ROPE_DIM:]], axis=-1)


def sliding_window_attention(q, k, v, sinks):
    B, S = q.shape[0], q.shape[1]
    nb = S // WINDOW
    qb = q.reshape(B, nb, WINDOW, N_KV_HEADS, Q_PER_KV, HEAD_DIM)

    def band(t):
        tb = t.reshape(B, nb, WINDOW, N_KV_HEADS, HEAD_DIM)
        prev = jnp.pad(tb, ((0, 0), (1, 0), (0, 0), (0, 0), (0, 0)))[:, :-1]
        return jnp.concatenate([prev, tb], axis=2)

    kb, vb = band(k), band(v)
    scores = jnp.einsum('bnqhgd,bnkhd->bnhgqk', qb, kb).astype(jnp.float32) * (HEAD_DIM ** -0.5)
    qi = jnp.arange(WINDOW)[:, None]
    kj = jnp.arange(2 * WINDOW)[None, :]
    rel = qi + WINDOW - kj
    band_ok = (rel >= 0) & (rel < WINDOW)
    blk_ok = (jnp.arange(nb)[:, None, None] > 0) | (kj >= WINDOW)[None]
    mask = band_ok[None] & blk_ok
    scores = jnp.where(mask[None, :, None, None], scores, NEG)
    sink = sinks.astype(jnp.float32).reshape(N_KV_HEADS, Q_PER_KV)[None, None, :, :, None, None]
    sink = jnp.broadcast_to(sink, scores.shape[:-1] + (1,))
    probs = jax.nn.softmax(jnp.concatenate([scores, sink], axis=-1), axis=-1)[..., :-1]
    out = jnp.einsum('bnhgqk,bnkhd->bnqhgd', probs.astype(v.dtype), vb)
    return out.reshape(B, S, ATTN_WIDTH)


def spatial_gating(z, ln_g, ln_b, w_s, b_s):
    B, S = z.shape[0], z.shape[1]
    u, v = jnp.split(z, 2, axis=-1)
    vf = v.astype(jnp.float32)
    mu = jnp.mean(vf, axis=-1, keepdims=True)
    var = jnp.mean(jnp.square(vf - mu), axis=-1, keepdims=True)
    vn = ((vf - mu) * lax.rsqrt(var + EPS) * ln_g.astype(jnp.float32) + ln_b.astype(jnp.float32)).astype(z.dtype)
    vc = vn.reshape(B, S // CHUNK, CHUNK, SGU_GROUPS, SGU_GROUP_DIM)
    causal = jnp.tril(jnp.ones((CHUNK, CHUNK), dtype=bool))
    w = jnp.where(causal[None], w_s, jnp.zeros_like(w_s))
    sv = jnp.einsum('gts,bnsgc->bntgc', w, vc) + b_s.T[None, None, :, :, None]
    return u * sv.reshape(B, S, SGU_WIDTH)


def setup_inputs(seed: int = 0) -> dict:
    key = jax.random.key(seed)
    ks = jax.random.split(key, 20)
    f32 = jnp.float32
    nrm = lambda k, shape, scale: jax.random.normal(k, shape, f32) * scale
    x = jax.random.normal(ks[0], (BATCH, SEQ, D_MODEL), f32)
    offset = jax.random.randint(ks[1], (BATCH, 1), 0, 4096, dtype=jnp.int32)
    positions = (jnp.arange(SEQ, dtype=jnp.int32)[None, :] + offset).astype(jnp.int32)
    return {
        "x": x,
        "positions": positions,
        "norm1_g": 1.0 + nrm(ks[2], (DEPTH, D_MODEL), 0.02),
        "w_in": nrm(ks[3], (DEPTH, D_MODEL, IN_WIDTH), D_MODEL ** -0.5),
        "b_in": nrm(ks[4], (DEPTH, IN_WIDTH), 0.02),
        "sinks": nrm(ks[5], (DEPTH, N_Q_HEADS), 1.0),
        "sgu_ln_g": 1.0 + nrm(ks[6], (DEPTH, SGU_WIDTH), 0.02),
        "sgu_ln_b": nrm(ks[7], (DEPTH, SGU_WIDTH), 0.02),
        "sgu_w": nrm(ks[8], (DEPTH, SGU_GROUPS, CHUNK, CHUNK), CHUNK ** -0.5),
        "sgu_b": 1.0 + nrm(ks[9], (DEPTH, SGU_GROUPS, CHUNK), 0.02),
        "w_attn_branch": nrm(ks[10], (DEPTH, ATTN_WIDTH, D_MODEL), ATTN_WIDTH ** -0.5),
        "w_sgu_branch": nrm(ks[11], (DEPTH, SGU_WIDTH, D_MODEL), SGU_WIDTH ** -0.5),
        "w_out": nrm(ks[12], (DEPTH, D_MODEL, D_MODEL), D_MODEL ** -0.5),
        "norm2_g": 1.0 + nrm(ks[13], (DEPTH, D_MODEL), 0.02),
        "w_gate_up": nrm(ks[14], (DEPTH, D_MODEL, 2 * D_FF), D_MODEL ** -0.5),
        "w_down": nrm(ks[15], (DEPTH, D_FF, D_MODEL), D_FF ** -0.5),
        "final_g": 1.0 + nrm(ks[16], (D_MODEL,), 0.02),
    }


def reference(x, positions, norm1_g, w_in, b_in, sinks, sgu_ln_g, sgu_ln_b, sgu_w, sgu_b,
              w_attn_branch, w_sgu_branch, w_out, norm2_g, w_gate_up, w_down, final_g):
    B, S = x.shape[0], x.shape[1]
    inv_freq = ROPE_THETA ** (-jnp.arange(0, ROPE_DIM, 2, dtype=jnp.float32) / ROPE_DIM)
    ang = positions.astype(jnp.float32)[..., None] * inv_freq
    cos = jnp.cos(ang)[:, :, None, :]
    sin = jnp.sin(ang)[:, :, None, :]
    h = x
    for l in range(DEPTH):
        xn = rmsnorm(h, norm1_g[l])
        proj = jnp.einsum('bsd,de->bse', xn, w_in[l]) + b_in[l]
        q = proj[..., OFF_Q:OFF_K].reshape(B, S, N_Q_HEADS, HEAD_DIM)
        k = proj[..., OFF_K:OFF_V].reshape(B, S, N_KV_HEADS, HEAD_DIM)
        v = proj[..., OFF_V:OFF_Z].reshape(B, S, N_KV_HEADS, HEAD_DIM)
        z = jax.nn.gelu(proj[..., OFF_Z:OFF_G], approximate=False)
        gates = jax.nn.sigmoid(proj[..., OFF_G:].astype(jnp.float32)).astype(h.dtype)
        g_attn, g_sgu = jnp.split(gates, 2, axis=-1)
        q = partial_rope(q, cos, sin)
        k = partial_rope(k, cos, sin)
        y_attn = sliding_window_attention(q, k, v, sinks[l])
        y_sgu = spatial_gating(z, sgu_ln_g[l], sgu_ln_b[l], sgu_w[l], sgu_b[l])
        merged = (g_attn * jnp.einsum('bse,ed->bsd', y_attn, w_attn_branch[l])
                  + g_sgu * jnp.einsum('bse,ed->bsd', y_sgu, w_sgu_branch[l]))
        h = h + jnp.einsum('bsd,de->bse', merged, w_out[l])
        hn = rmsnorm(h, norm2_g[l])
        gu = jnp.einsum('bsd,df->bsf', hn, w_gate_up[l])
        gate, up = jnp.split(gu, 2, axis=-1)
        h = h + jnp.einsum('bsf,fd->bsd', jax.nn.silu(gate) * up, w_down[l])
    return rmsnorm(h, final_g)
```

```python
import functools

import jax
import jax.numpy as jnp
from jax import lax
from jax.experimental import pallas as pl
from jax.experimental.pallas import tpu as pltpu

D_MODEL = 2048
N_Q_HEADS = 16
N_KV_HEADS = 4
HEAD_DIM = 64
Q_PER_KV = N_Q_HEADS // N_KV_HEADS
ATTN_WIDTH = N_Q_HEADS * HEAD_DIM
KV_WIDTH = N_KV_HEADS * HEAD_DIM
WINDOW = 128
ROPE_THETA = 500000.0
ROPE_DIM = HEAD_DIM // 4
ROPE_HALF = ROPE_DIM // 2
SGU_WIDTH = D_MODEL // 2
SGU_GROUPS = 8
SGU_GROUP_DIM = SGU_WIDTH // SGU_GROUPS
CHUNK = 128
D_FF = 5632
EPS = 1e-5
NEG = -1e30

COL_GA = 0
COL_GS = COL_GA + D_MODEL
COL_U = COL_GS + D_MODEL
COL_SV = COL_U + SGU_WIDTH
COL_Q = COL_SV + SGU_WIDTH
COL_K = COL_Q + ATTN_WIDTH
COL_V = COL_K + KV_WIDTH
IN_WIDTH = COL_V + KV_WIDTH

LANES = 128
VMEM_LIMIT = 52 * 1024 * 1024

BF16 = jnp.bfloat16
F32 = jnp.float32


def _params(n_axes):
    return pltpu.CompilerParams(dimension_semantics=("arbitrary",) * n_axes,
                                vmem_limit_bytes=VMEM_LIMIT)


def _rmsnorm_rows(x, g):
    return x * lax.rsqrt(jnp.mean(x * x, axis=-1, keepdims=True) + EPS) * g


def _rope_table_kernel(pos_ref, invf_ref, c_ref, s1_ref, s2_ref):
    lane = lax.broadcasted_iota(jnp.int32, c_ref.shape, 1) % HEAD_DIM
    ang = pos_ref[...].astype(F32) * invf_ref[...]
    cos, sin = jnp.cos(ang), jnp.sin(ang)
    first, second = lane < ROPE_HALF, (lane >= ROPE_HALF) & (lane < ROPE_DIM)
    c_ref[...] = jnp.where(lane < ROPE_DIM, cos, 1.0)
    s1_ref[...] = jnp.where(first, -sin, 0.0)
    s2_ref[...] = jnp.where(second, sin, 0.0)


def _rope_tables(pos, invf_lanes, tm=1024):
    T = pos.shape[0]
    tab = jax.ShapeDtypeStruct((T, LANES), F32)
    spec = pl.BlockSpec((tm, LANES), lambda i: (i, 0))
    return pl.pallas_call(
        _rope_table_kernel, grid=(T // tm,),
        in_specs=[pl.BlockSpec((tm, 1), lambda i: (i, 0)),
                  pl.BlockSpec((1, LANES), lambda i: (0, 0))],
        out_specs=[spec, spec, spec], out_shape=[tab, tab, tab],
        compiler_params=_params(1), name="rope_tables")(pos, invf_lanes)


def _norm_kernel(x_ref, g_ref, o_ref):
    o_ref[...] = _rmsnorm_rows(x_ref[...], g_ref[...]).astype(o_ref.dtype)


def _rmsnorm(x, g, tm=512):
    T, D = x.shape
    return pl.pallas_call(
        _norm_kernel, grid=(T // tm,),
        in_specs=[pl.BlockSpec((tm, D), lambda i: (i, 0)),
                  pl.BlockSpec((1, D), lambda i: (0, 0))],
        out_specs=pl.BlockSpec((tm, D), lambda i: (i, 0)),
        out_shape=jax.ShapeDtypeStruct((T, D), BF16),
        compiler_params=_params(1), name="rmsnorm0")(x, g)


IN_TN = 512
J_Z = COL_U // IN_TN
J_Q = COL_Q // IN_TN
J_KV = COL_K // IN_TN


def _rope(r, c, s1, s2):
    return (r * c + pltpu.roll(r, LANES - ROPE_HALF, 1) * s1
            + pltpu.roll(r, ROPE_HALF, 1) * s2)


def _in_proj_kernel(xn_ref, w_ref, b_ref, c_ref, s1_ref, s2_ref, o_ref):
    j = pl.program_id(1)
    r = jnp.dot(xn_ref[...], w_ref[...], preferred_element_type=F32) + b_ref[...]

    @pl.when(j < J_Z)
    def _():
        o_ref[...] = (1.0 / (1.0 + jnp.exp(-r))).astype(o_ref.dtype)

    @pl.when((j >= J_Z) & (j < J_Q))
    def _():
        o_ref[...] = (0.5 * r * (1.0 + lax.erf(r * (2.0 ** -0.5)))).astype(o_ref.dtype)

    def rope_cols(n_groups):
        c, s1, s2 = c_ref[...], s1_ref[...], s2_ref[...]
        for gi in range(n_groups):
            sl = slice(gi * LANES, (gi + 1) * LANES)
            o_ref[:, sl] = _rope(r[:, sl], c, s1, s2).astype(o_ref.dtype)

    @pl.when((j >= J_Q) & (j < J_KV))
    def _():
        rope_cols(IN_TN // LANES)

    @pl.when(j == J_KV)
    def _():
        rope_cols(KV_WIDTH // LANES)
        o_ref[:, KV_WIDTH:] = r[:, KV_WIDTH:].astype(o_ref.dtype)


def _in_proj(xn, w, b, tabs, tm=1024):
    T = xn.shape[0]
    tab_spec = pl.BlockSpec((tm, LANES), lambda i, j: (i, 0))
    return pl.pallas_call(
        _in_proj_kernel, grid=(T // tm, IN_WIDTH // IN_TN),
        in_specs=[pl.BlockSpec((tm, D_MODEL), lambda i, j: (i, 0)),
                  pl.BlockSpec((D_MODEL, IN_TN), lambda i, j: (0, j)),
                  pl.BlockSpec((1, IN_TN), lambda i, j: (0, j)),
                  tab_spec, tab_spec, tab_spec],
        out_specs=pl.BlockSpec((tm, IN_TN), lambda i, j: (i, j)),
        out_shape=jax.ShapeDtypeStruct((T, IN_WIDTH), BF16),
        compiler_params=_params(2), name="in_proj")(xn, w, b, *tabs)


ATT_TQ = 512
ATT_SUB = ATT_TQ // WINDOW


def _attn_kernel(blocks_per_seq, sinks_ref, q_ref, kc_ref, vc_ref, kp_ref, vp_ref, o_ref):
    t = pl.program_id(0)
    qi = lax.broadcasted_iota(jnp.int32, (WINDOW, 2 * WINDOW), 0)
    kj = lax.broadcasted_iota(jnp.int32, (WINDOW, 2 * WINDOW), 1)
    rel = qi + WINDOW - kj
    band_ok = (rel >= 0) & (rel < WINDOW)
    for c in range(ATT_SUB):
        rows = slice(c * WINDOW, (c + 1) * WINDOW)
        if c == 0:
            first_key = jnp.where((t * ATT_SUB) % blocks_per_seq == 0, WINDOW, 0)
            mask = band_ok & (kj >= first_key)
            k_prev, v_prev = kp_ref[...], vp_ref[...]
        else:
            mask = band_ok
            prev_rows = slice((c - 1) * WINDOW, c * WINDOW)
            k_prev, v_prev = kc_ref[prev_rows, :], vc_ref[prev_rows, :]
        k_band = jnp.concatenate([k_prev, kc_ref[rows, :]], axis=0)
        v_band = jnp.concatenate([v_prev, vc_ref[rows, :]], axis=0)
        q = q_ref[rows, :]
        for g in range(N_KV_HEADS):
            kv_cols = slice(g * HEAD_DIM, (g + 1) * HEAD_DIM)
            kg, vg = k_band[:, kv_cols], v_band[:, kv_cols]
            outs = []
            for i in range(Q_PER_KV):
                h = g * Q_PER_KV + i
                qh = q[:, h * HEAD_DIM:(h + 1) * HEAD_DIM]
                s = lax.dot_general(qh, kg, (((1,), (1,)), ((), ())),
                                    preferred_element_type=F32) * (HEAD_DIM ** -0.5)
                s = jnp.where(mask, s, NEG)
                sink = sinks_ref[h]
                m = jnp.maximum(jnp.max(s, axis=-1, keepdims=True), sink)
                p = jnp.exp(s - m)
                denom = jnp.sum(p, axis=-1, keepdims=True) + jnp.exp(sink - m)
                pv = jnp.dot(p.astype(BF16), vg, preferred_element_type=F32)
                outs.append(pv / denom)
            o_ref[rows, g * Q_PER_KV * HEAD_DIM:(g + 1) * Q_PER_KV * HEAD_DIM] = (
                jnp.concatenate(outs, axis=1).astype(o_ref.dtype))


def _attention(proj, sinks, seq_len):
    T = proj.shape[0]
    kcol, vcol = COL_K // KV_WIDTH, COL_V // KV_WIDTH
    prev = lambda t: jnp.maximum(t * ATT_SUB - 1, 0)
    return pl.pallas_call(
        functools.partial(_attn_kernel, seq_len // WINDOW), grid=(T // ATT_TQ,),
        in_specs=[pl.BlockSpec(memory_space=pltpu.SMEM),
                  pl.BlockSpec((ATT_TQ, ATTN_WIDTH), lambda t: (t, COL_Q // ATTN_WIDTH)),
                  pl.BlockSpec((ATT_TQ, KV_WIDTH), lambda t: (t, kcol)),
                  pl.BlockSpec((ATT_TQ, KV_WIDTH), lambda t: (t, vcol)),
                  pl.BlockSpec((WINDOW, KV_WIDTH), lambda t: (prev(t), kcol)),
                  pl.BlockSpec((WINDOW, KV_WIDTH), lambda t: (prev(t), vcol))],
        out_specs=pl.BlockSpec((ATT_TQ, ATTN_WIDTH), lambda t: (t, 0)),
        out_shape=jax.ShapeDtypeStruct((T, ATTN_WIDTH), BF16),
        compiler_params=_params(1), name="swa_attention")(sinks, proj, proj, proj, proj, proj)


SGU_TQ = 512


def _sgu_kernel(u_ref, v_ref, lng_ref, lnb_ref, w_ref, b_ref, o_ref):
    v = v_ref[...].astype(F32)
    mu = jnp.mean(v, axis=-1, keepdims=True)
    d = v - mu
    var = jnp.mean(d * d, axis=-1, keepdims=True)
    vn = (d * lax.rsqrt(var + EPS) * lng_ref[...] + lnb_ref[...]).astype(BF16)
    ti = lax.broadcasted_iota(jnp.int32, (CHUNK, CHUNK), 0)
    si = lax.broadcasted_iota(jnp.int32, (CHUNK, CHUNK), 1)
    causal = si <= ti
    for g in range(SGU_GROUPS):
        cols = slice(g * SGU_GROUP_DIM, (g + 1) * SGU_GROUP_DIM)
        w = jnp.where(causal, w_ref[g], jnp.zeros_like(w_ref[g]))
        bias = b_ref[g]
        for c in range(SGU_TQ // CHUNK):
            rows = slice(c * CHUNK, (c + 1) * CHUNK)
            sv = jnp.dot(w, vn[rows, cols], preferred_element_type=F32) + bias
            o_ref[rows, cols] = (u_ref[rows, cols].astype(F32) * sv).astype(o_ref.dtype)


def _spatial_gating(proj, ln_g, ln_b, w, b):
    T = proj.shape[0]
    full = lambda shape: pl.BlockSpec(shape, lambda t: (0,) * len(shape))
    return pl.pallas_call(
        _sgu_kernel, grid=(T // SGU_TQ,),
        in_specs=[pl.BlockSpec((SGU_TQ, SGU_WIDTH), lambda t: (t, COL_U // SGU_WIDTH)),
                  pl.BlockSpec((SGU_TQ, SGU_WIDTH), lambda t: (t, COL_SV // SGU_WIDTH)),
                  full((1, SGU_WIDTH)), full((1, SGU_WIDTH)),
                  full((SGU_GROUPS, CHUNK, CHUNK)), full((SGU_GROUPS, CHUNK, 1))],
        out_specs=pl.BlockSpec((SGU_TQ, SGU_WIDTH), lambda t: (t, 0)),
        out_shape=jax.ShapeDtypeStruct((T, SGU_WIDTH), BF16),
        compiler_params=_params(1), name="spatial_gating")(proj, proj, ln_g, ln_b, w, b)


MRG_TN = 512


def _merge_kernel(ya_ref, ys_ref, ga_ref, gs_ref, h_ref, wa_ref, ws_ref, wo_ref, g2_ref,
                  h1_ref, hn_ref):
    j = pl.program_id(1)
    a = jnp.dot(ya_ref[...], wa_ref[...], preferred_element_type=F32)
    s = jnp.dot(ys_ref[...], ws_ref[...], preferred_element_type=F32)
    merged = (ga_ref[...].astype(F32) * a + gs_ref[...].astype(F32) * s).astype(BF16)
    contrib = jnp.dot(merged, wo_ref[...], preferred_element_type=F32)

    @pl.when(j == 0)
    def _():
        h1_ref[...] = h_ref[...] + contrib

    @pl.when(j > 0)
    def _():
        h1_ref[...] += contrib

    @pl.when(j == pl.num_programs(1) - 1)
    def _():
        hn_ref[...] = _rmsnorm_rows(h1_ref[...], g2_ref[...]).astype(hn_ref.dtype)


def _merge(y_attn, y_sgu, proj, h, w_ab, w_sb, w_out, g2, tm=512):
    T = h.shape[0]
    row = lambda width: pl.BlockSpec((tm, width), lambda i, j: (i, 0))
    return pl.pallas_call(
        _merge_kernel, grid=(T // tm, D_MODEL // MRG_TN),
        in_specs=[row(ATTN_WIDTH), row(SGU_WIDTH),
                  pl.BlockSpec((tm, MRG_TN), lambda i, j: (i, COL_GA // MRG_TN + j)),
                  pl.BlockSpec((tm, MRG_TN), lambda i, j: (i, COL_GS // MRG_TN + j)),
                  row(D_MODEL),
                  pl.BlockSpec((ATTN_WIDTH, MRG_TN), lambda i, j: (0, j)),
                  pl.BlockSpec((SGU_WIDTH, MRG_TN), lambda i, j: (0, j)),
                  pl.BlockSpec((MRG_TN, D_MODEL), lambda i, j: (j, 0)),
                  pl.BlockSpec((1, D_MODEL), lambda i, j: (0, 0))],
        out_specs=[row(D_MODEL), row(D_MODEL)],
        out_shape=[jax.ShapeDtypeStruct((T, D_MODEL), F32),
                   jax.ShapeDtypeStruct((T, D_MODEL), BF16)],
        compiler_params=_params(2), name="merge_out_proj")(
            y_attn, y_sgu, proj, proj, h, w_ab, w_sb, w_out, g2)


FFN_TF = 512


def _ffn_kernel(hn_ref, h1_ref, wg_ref, wu_ref, wd_ref, gn_ref, h2_ref, xn_ref):
    f = pl.program_id(1)
    hn = hn_ref[...]
    gate = jnp.dot(hn, wg_ref[...], preferred_element_type=F32)
    up = jnp.dot(hn, wu_ref[...], preferred_element_type=F32)
    act = (gate * (1.0 / (1.0 + jnp.exp(-gate))) * up).astype(BF16)
    contrib = jnp.dot(act, wd_ref[...], preferred_element_type=F32)

    @pl.when(f == 0)
    def _():
        h2_ref[...] = h1_ref[...] + contrib

    @pl.when(f > 0)
    def _():
        h2_ref[...] += contrib

    @pl.when(f == pl.num_programs(1) - 1)
    def _():
        xn_ref[...] = _rmsnorm_rows(h2_ref[...], gn_ref[...]).astype(xn_ref.dtype)


def _ffn(hn, h1, w_gu, w_down, g_next, out_dtype, tm=512):
    T = h1.shape[0]
    nf = D_FF // FFN_TF
    row = pl.BlockSpec((tm, D_MODEL), lambda i, f: (i, 0))
    return pl.pallas_call(
        _ffn_kernel, grid=(T // tm, nf),
        in_specs=[row, row,
                  pl.BlockSpec((D_MODEL, FFN_TF), lambda i, f: (0, f)),
                  pl.BlockSpec((D_MODEL, FFN_TF), lambda i, f: (0, nf + f)),
                  pl.BlockSpec((FFN_TF, D_MODEL), lambda i, f: (f, 0)),
                  pl.BlockSpec((1, D_MODEL), lambda i, f: (0, 0))],
        out_specs=[row, row],
        out_shape=[jax.ShapeDtypeStruct((T, D_MODEL), F32),
                   jax.ShapeDtypeStruct((T, D_MODEL), out_dtype)],
        compiler_params=_params(2), name="swiglu_ffn")(hn, h1, w_gu, w_gu, w_down, g_next)


def _reorder_in_cols(a):
    off_z = ATTN_WIDTH + 2 * KV_WIDTH
    off_g = off_z + 2 * SGU_WIDTH
    return jnp.concatenate([a[..., off_g:], a[..., off_z:off_g], a[..., :off_z]], axis=-1)


def kernel(x, positions, norm1_g, w_in, b_in, sinks, sgu_ln_g, sgu_ln_b, sgu_w, sgu_b,
           w_attn_branch, w_sgu_branch, w_out, norm2_g, w_gate_up, w_down, final_g):
    B, S, D = x.shape
    T = B * S
    depth = w_in.shape[0]
    assert D == D_MODEL and S % ATT_TQ == 0 and w_in.shape[-1] == IN_WIDTH

    inv_freq = ROPE_THETA ** (-jnp.arange(0, ROPE_DIM, 2, dtype=F32) / ROPE_DIM)
    lane = jnp.arange(LANES) % HEAD_DIM
    invf_lanes = jnp.where(lane < ROPE_DIM, inv_freq[lane % ROPE_HALF], 0.0)[None, :]
    tabs = _rope_tables(positions.reshape(T, 1), invf_lanes)

    w_in_b = _reorder_in_cols(w_in).astype(BF16)
    b_in_r = _reorder_in_cols(b_in)[:, None, :]
    w_ab, w_sb, w_o = (w.astype(BF16) for w in (w_attn_branch, w_sgu_branch, w_out))
    w_gu, w_dn = w_gate_up.astype(BF16), w_down.astype(BF16)
    sgu_w_b = sgu_w.astype(BF16)
    sgu_b_c = sgu_b[..., None]

    h = x.reshape(T, D)
    xn = _rmsnorm(h, norm1_g[0][None, :])
    for l in range(depth):
        proj = _in_proj(xn, w_in_b[l], b_in_r[l], tabs)
        y_attn = _attention(proj, sinks[l], S)
        y_sgu = _spatial_gating(proj, sgu_ln_g[l][None, :], sgu_ln_b[l][None, :],
                                sgu_w_b[l], sgu_b_c[l])
        h1, hn = _merge(y_attn, y_sgu, proj, h, w_ab[l], w_sb[l], w_o[l], norm2_g[l][None, :])
        last = l == depth - 1
        g_next = final_g if last else norm1_g[l + 1]
        h, xn = _ffn(hn, h1, w_gu[l], w_dn[l], g_next[None, :], F32 if last else BF16)
    return xn.reshape(B, S, D)
```

```python
import functools

import jax
import jax.numpy as jnp
from jax import lax
from jax.experimental import pallas as pl
from jax.experimental.pallas import tpu as pltpu

D_MODEL = 2048
N_Q_HEADS = 16
N_KV_HEADS = 4
HEAD_DIM = 64
Q_PER_KV = N_Q_HEADS // N_KV_HEADS
ATTN_WIDTH = N_Q_HEADS * HEAD_DIM
KV_WIDTH = N_KV_HEADS * HEAD_DIM
WINDOW = 128
ROPE_THETA = 500000.0
ROPE_DIM = HEAD_DIM // 4
ROPE_HALF = ROPE_DIM // 2
SGU_WIDTH = D_MODEL // 2
SGU_GROUPS = 8
SGU_GROUP_DIM = SGU_WIDTH // SGU_GROUPS
CHUNK = 128
D_FF = 5632
EPS = 1e-5
NEG = -1e30

OFF_Q = 0
OFF_K = OFF_Q + ATTN_WIDTH
OFF_Z = OFF_K + 2 * KV_WIDTH
OFF_G = OFF_Z + 2 * SGU_WIDTH
IN_WIDTH = OFF_G + 2 * D_MODEL

LANES = 128
VMEM_LIMIT = 52 * 1024 * 1024

BF16 = jnp.bfloat16
F32 = jnp.float32


def _params(n_axes):
    return pltpu.CompilerParams(dimension_semantics=("arbitrary",) * n_axes,
                                vmem_limit_bytes=VMEM_LIMIT)


def _rmsnorm_rows(x, g):
    return x * lax.rsqrt(jnp.mean(x * x, axis=-1, keepdims=True) + EPS) * g


def _low_half_lanes(shape=(1, LANES)):
    return lax.broadcasted_iota(jnp.int32, shape, len(shape) - 1) % LANES < HEAD_DIM


def _rope_table_kernel(pos_ref, invf_ref, c_ref, s1_ref, s2_ref):
    lane = lax.broadcasted_iota(jnp.int32, c_ref.shape, 1) % HEAD_DIM
    ang = pos_ref[...].astype(F32) * invf_ref[...]
    cos, sin = jnp.cos(ang), jnp.sin(ang)
    first, second = lane < ROPE_HALF, (lane >= ROPE_HALF) & (lane < ROPE_DIM)
    c_ref[...] = jnp.where(lane < ROPE_DIM, cos, 1.0)
    s1_ref[...] = jnp.where(first, -sin, 0.0)
    s2_ref[...] = jnp.where(second, sin, 0.0)


def _rope_tables(pos, invf_lanes, tm=1024):
    T = pos.shape[0]
    tab = jax.ShapeDtypeStruct((T, LANES), F32)
    spec = pl.BlockSpec((tm, LANES), lambda i: (i, 0))
    return pl.pallas_call(
        _rope_table_kernel, grid=(T // tm,),
        in_specs=[pl.BlockSpec((tm, 1), lambda i: (i, 0)),
                  pl.BlockSpec((1, LANES), lambda i: (0, 0))],
        out_specs=[spec, spec, spec], out_shape=[tab, tab, tab],
        compiler_params=_params(1), name="rope_tables")(pos, invf_lanes)


def _norm_kernel(x_ref, g_ref, o_ref):
    o_ref[...] = _rmsnorm_rows(x_ref[...], g_ref[...]).astype(o_ref.dtype)


def _rmsnorm(x, g, tm=512):
    T, D = x.shape
    return pl.pallas_call(
        _norm_kernel, grid=(T // tm,),
        in_specs=[pl.BlockSpec((tm, D), lambda i: (i, 0)),
                  pl.BlockSpec((1, D), lambda i: (0, 0))],
        out_specs=pl.BlockSpec((tm, D), lambda i: (i, 0)),
        out_shape=jax.ShapeDtypeStruct((T, D), BF16),
        compiler_params=_params(1), name="rmsnorm0")(x, g)


PROJ_TM = 1024
PROJ_SUB = 256


def _rope(r, c, s1, s2):
    return (r * c + pltpu.roll(r, LANES - ROPE_HALF, 1) * s1
            + pltpu.roll(r, ROPE_HALF, 1) * s2)


def _sub_tiles(xn_ref, w_ref, b_ref):
    xn = xn_ref[...]
    for si in range(w_ref.shape[1] // PROJ_SUB):
        sl = slice(si * PROJ_SUB, (si + 1) * PROJ_SUB)
        yield si, jnp.dot(xn, w_ref[:, sl], preferred_element_type=F32) + b_ref[:, sl]


def _gates_kernel(xn_ref, w_ref, b_ref, o_ref):
    for si, r in _sub_tiles(xn_ref, w_ref, b_ref):
        o_ref[:, si * PROJ_SUB:(si + 1) * PROJ_SUB] = (1.0 / (1.0 + jnp.exp(-r))).astype(o_ref.dtype)


def _gelu_kernel(xn_ref, w_ref, b_ref, o_ref):
    for si, r in _sub_tiles(xn_ref, w_ref, b_ref):
        o_ref[:, si * PROJ_SUB:(si + 1) * PROJ_SUB] = (
            0.5 * r * (1.0 + lax.erf(r * (2.0 ** -0.5)))).astype(o_ref.dtype)


def _q_kernel(xn_ref, w_ref, b_ref, c_ref, s1_ref, s2_ref, o_ref):
    c, s1, s2 = (t[...] * (HEAD_DIM ** -0.5) for t in (c_ref, s1_ref, s2_ref))
    for si, r in _sub_tiles(xn_ref, w_ref, b_ref):
        for gi in range(PROJ_SUB // LANES):
            x = r[:, gi * LANES:(gi + 1) * LANES]
            lo = si * PROJ_SUB + gi * LANES
            o_ref[:, lo:lo + LANES] = _rope(x, c, s1, s2).astype(o_ref.dtype)


def _kv_kernel(xn_ref, w_ref, b_ref, c_ref, s1_ref, s2_ref, o_ref):
    c, s1, s2 = c_ref[...], s1_ref[...], s2_ref[...]
    low = _low_half_lanes()
    for si, r in _sub_tiles(xn_ref, w_ref, b_ref):
        for gi in range(PROJ_SUB // LANES):
            x = r[:, gi * LANES:(gi + 1) * LANES]
            if si * PROJ_SUB < KV_WIDTH:
                x = _rope(x, c, s1, s2)
            swapped = pltpu.roll(x, HEAD_DIM, 1)
            lo = 2 * (si * PROJ_SUB + gi * LANES)
            o_ref[:, lo:lo + LANES] = jnp.where(low, x, swapped).astype(o_ref.dtype)
            o_ref[:, lo + LANES:lo + 2 * LANES] = jnp.where(low, swapped, x).astype(o_ref.dtype)


def _proj(body, name, xn, w, b, tn, out_scale=1, tabs=()):
    T, tm = xn.shape[0], PROJ_TM
    width = w.shape[1]
    tab_spec = pl.BlockSpec((tm, LANES), lambda i, j: (i, 0))
    return pl.pallas_call(
        body, grid=(T // tm, width // tn),
        in_specs=[pl.BlockSpec((tm, D_MODEL), lambda i, j: (i, 0)),
                  pl.BlockSpec((D_MODEL, tn), lambda i, j: (0, j)),
                  pl.BlockSpec((1, tn), lambda i, j: (0, j))] + [tab_spec] * len(tabs),
        out_specs=pl.BlockSpec((tm, out_scale * tn), lambda i, j: (i, j)),
        out_shape=jax.ShapeDtypeStruct((T, out_scale * width), BF16),
        compiler_params=_params(2), name=name)(xn, w, b, *tabs)


ATT_TQ = 512
ATT_SUB = ATT_TQ // WINDOW
KVDUP_WIDTH = 2 * KV_WIDTH


def _attn_kernel(blocks_per_seq, sinks_ref, q_ref, kc_ref, vc_ref, kp_ref, vp_ref, o_ref):
    t = pl.program_id(0)
    qi = lax.broadcasted_iota(jnp.int32, (WINDOW, 2 * WINDOW), 0)
    kj = lax.broadcasted_iota(jnp.int32, (WINDOW, 2 * WINDOW), 1)
    rel = qi + WINDOW - kj
    band_ok = (rel >= 0) & (rel < WINDOW)
    low = _low_half_lanes()
    ones_low = jnp.broadcast_to(jnp.where(low, 1.0, 0.0).astype(BF16), (2 * WINDOW, LANES))
    ones_high = jnp.broadcast_to(jnp.where(low, 0.0, 1.0).astype(BF16), (2 * WINDOW, LANES))
    for c in range(ATT_SUB):
        rows = slice(c * WINDOW, (c + 1) * WINDOW)
        if c == 0:
            first_key = jnp.where((t * ATT_SUB) % blocks_per_seq == 0, WINDOW, 0)
            mask = band_ok & (kj >= first_key)
            k_prev, v_prev = kp_ref[...], vp_ref[...]
        else:
            mask = band_ok
            prev_rows = slice((c - 1) * WINDOW, c * WINDOW)
            k_prev, v_prev = kc_ref[prev_rows, :], vc_ref[prev_rows, :]
        k_band = jnp.concatenate([k_prev, kc_ref[rows, :]], axis=0)
        v_band = jnp.concatenate([v_prev, vc_ref[rows, :]], axis=0)
        for g in range(N_KV_HEADS):
            lanes_g = slice(g * LANES, (g + 1) * LANES)
            kg, vg = k_band[:, lanes_g], v_band[:, lanes_g]
            zero = jnp.zeros_like(vg)
            rhs_pv = jnp.concatenate(
                [jnp.concatenate([jnp.where(low, vg, zero), ones_low], axis=1),
                 jnp.concatenate([jnp.where(low, zero, vg), ones_high], axis=1)], axis=0)
            q_pairs = [q_ref[rows, (2 * g + i) * LANES:(2 * g + i + 1) * LANES] for i in range(2)]
            zq = jnp.zeros_like(q_pairs[0])
            lhs = jnp.concatenate([jnp.where(low, q_pairs[0], zq), jnp.where(low, zq, q_pairs[0]),
                                   jnp.where(low, q_pairs[1], zq), jnp.where(low, zq, q_pairs[1])],
                                  axis=0)
            s = lax.dot_general(lhs, kg, (((1,), (1,)), ((), ())), preferred_element_type=F32)
            probs, sink_terms = [], []
            for i in range(Q_PER_KV):
                sink = sinks_ref[g * Q_PER_KV + i]
                si = jnp.where(mask, s[i * WINDOW:(i + 1) * WINDOW, :], NEG)
                m = jnp.maximum(jnp.max(si, axis=-1, keepdims=True), sink)
                probs.append(jnp.exp(si - m).astype(BF16))
                sink_terms.append(jnp.exp(sink - m))
            for pair in range(2):
                p2 = jnp.concatenate(probs[2 * pair:2 * pair + 2], axis=1)
                r = jnp.dot(p2, rhs_pv, preferred_element_type=F32)
                den = r[:, LANES:] + jnp.where(low, sink_terms[2 * pair], sink_terms[2 * pair + 1])
                out_lanes = slice((2 * g + pair) * LANES, (2 * g + pair + 1) * LANES)
                o_ref[rows, out_lanes] = (r[:, :LANES] / den).astype(o_ref.dtype)


def _attention(q, kv, sinks, seq_len):
    T = q.shape[0]
    prev = lambda t: jnp.maximum(t * ATT_SUB - 1, 0)
    return pl.pallas_call(
        functools.partial(_attn_kernel, seq_len // WINDOW), grid=(T // ATT_TQ,),
        in_specs=[pl.BlockSpec(memory_space=pltpu.SMEM),
                  pl.BlockSpec((ATT_TQ, ATTN_WIDTH), lambda t: (t, 0)),
                  pl.BlockSpec((ATT_TQ, KVDUP_WIDTH), lambda t: (t, 0)),
                  pl.BlockSpec((ATT_TQ, KVDUP_WIDTH), lambda t: (t, 1)),
                  pl.BlockSpec((WINDOW, KVDUP_WIDTH), lambda t: (prev(t), 0)),
                  pl.BlockSpec((WINDOW, KVDUP_WIDTH), lambda t: (prev(t), 1))],
        out_specs=pl.BlockSpec((ATT_TQ, ATTN_WIDTH), lambda t: (t, 0)),
        out_shape=jax.ShapeDtypeStruct((T, ATTN_WIDTH), BF16),
        compiler_params=_params(1), name="swa_attention")(sinks, q, kv, kv, kv, kv)


SGU_TQ = 512


def _sgu_kernel(u_ref, v_ref, lng_ref, lnb_ref, w_ref, b_ref, o_ref):
    v = v_ref[...].astype(F32)
    mu = jnp.mean(v, axis=-1, keepdims=True)
    d = v - mu
    var = jnp.mean(d * d, axis=-1, keepdims=True)
    vn = (d * lax.rsqrt(var + EPS) * lng_ref[...] + lnb_ref[...]).astype(BF16)
    ti = lax.broadcasted_iota(jnp.int32, (CHUNK, CHUNK), 0)
    si = lax.broadcasted_iota(jnp.int32, (CHUNK, CHUNK), 1)
    causal = si <= ti
    for g in range(SGU_GROUPS):
        cols = slice(g * SGU_GROUP_DIM, (g + 1) * SGU_GROUP_DIM)
        w = jnp.where(causal, w_ref[g], jnp.zeros_like(w_ref[g]))
        bias = b_ref[g]
        for c in range(SGU_TQ // CHUNK):
            rows = slice(c * CHUNK, (c + 1) * CHUNK)
            sv = jnp.dot(w, vn[rows, cols], preferred_element_type=F32) + bias
            o_ref[rows, cols] = (u_ref[rows, cols].astype(F32) * sv).astype(o_ref.dtype)


def _spatial_gating(z, ln_g, ln_b, w, b):
    T = z.shape[0]
    full = lambda shape: pl.BlockSpec(shape, lambda t: (0,) * len(shape))
    return pl.pallas_call(
        _sgu_kernel, grid=(T // SGU_TQ,),
        in_specs=[pl.BlockSpec((SGU_TQ, SGU_WIDTH), lambda t: (t, 0)),
                  pl.BlockSpec((SGU_TQ, SGU_WIDTH), lambda t: (t, 1)),
                  full((1, SGU_WIDTH)), full((1, SGU_WIDTH)),
                  full((SGU_GROUPS, CHUNK, CHUNK)), full((SGU_GROUPS, CHUNK, 1))],
        out_specs=pl.BlockSpec((SGU_TQ, SGU_WIDTH), lambda t: (t, 0)),
        out_shape=jax.ShapeDtypeStruct((T, SGU_WIDTH), BF16),
        compiler_params=_params(1), name="spatial_gating")(z, z, ln_g, ln_b, w, b)


MRG_TN = 512


def _merge_kernel(ya_ref, ys_ref, g_ref, h_ref, wa_ref, ws_ref, wo_ref, g2_ref,
                  h1_ref, hn_ref, merged_ref):
    ya, ys = ya_ref[...], ys_ref[...]
    for j in range(D_MODEL // MRG_TN):
        cols = slice(j * MRG_TN, (j + 1) * MRG_TN)
        gs_cols = slice(D_MODEL + j * MRG_TN, D_MODEL + (j + 1) * MRG_TN)
        a = jnp.dot(ya, wa_ref[:, cols], preferred_element_type=F32)
        s = jnp.dot(ys, ws_ref[:, cols], preferred_element_type=F32)
        merged_ref[:, cols] = (g_ref[:, cols].astype(F32) * a
                               + g_ref[:, gs_cols].astype(F32) * s).astype(BF16)
    merged = merged_ref[...]
    sum_sq = jnp.zeros((merged.shape[0], 1), F32)
    for j in range(D_MODEL // MRG_TN):
        cols = slice(j * MRG_TN, (j + 1) * MRG_TN)
        h1 = h_ref[:, cols] + jnp.dot(merged, wo_ref[:, cols], preferred_element_type=F32)
        h1_ref[:, cols] = h1
        sum_sq += jnp.sum(h1 * h1, axis=-1, keepdims=True)
    scale = lax.rsqrt(sum_sq * (1.0 / D_MODEL) + EPS)
    hn_ref[...] = (h1_ref[...] * scale * g2_ref[...]).astype(hn_ref.dtype)


def _merge(y_attn, y_sgu, gates, h, w_ab, w_sb, w_out, g2, tm=256):
    T = h.shape[0]
    row = lambda width: pl.BlockSpec((tm, width), lambda i: (i, 0))
    resident = lambda shape: pl.BlockSpec(shape, lambda i: (0, 0), pipeline_mode=pl.Buffered(1))
    return pl.pallas_call(
        _merge_kernel, grid=(T // tm,),
        in_specs=[row(ATTN_WIDTH), row(SGU_WIDTH), row(2 * D_MODEL), row(D_MODEL),
                  resident((ATTN_WIDTH, D_MODEL)), resident((SGU_WIDTH, D_MODEL)),
                  resident((D_MODEL, D_MODEL)), resident((1, D_MODEL))],
        out_specs=[row(D_MODEL), row(D_MODEL)],
        out_shape=[jax.ShapeDtypeStruct((T, D_MODEL), F32),
                   jax.ShapeDtypeStruct((T, D_MODEL), BF16)],
        scratch_shapes=[pltpu.VMEM((tm, D_MODEL), BF16)],
        compiler_params=_params(1), name="merge_out_proj")(
            y_attn, y_sgu, gates, h, w_ab, w_sb, w_out, g2)


FFN_TF = 512
FFN_SUB = 256
FFN_DOWN_TN = 512


def _ffn_kernel(hn_ref, h1_ref, wg_ref, wu_ref, wd_ref, gn_ref, h2_ref, xn_ref):
    f = pl.program_id(1)

    @pl.when(f == 0)
    def _():
        h2_ref[...] = h1_ref[...]

    hn = hn_ref[...]
    acts = []
    for si in range(FFN_TF // FFN_SUB):
        sl = slice(si * FFN_SUB, (si + 1) * FFN_SUB)
        gate = jnp.dot(hn, wg_ref[:, sl], preferred_element_type=F32)
        up = jnp.dot(hn, wu_ref[:, sl], preferred_element_type=F32)
        acts.append((gate * (1.0 / (1.0 + jnp.exp(-gate))) * up).astype(BF16))
    act = jnp.concatenate(acts, axis=1)
    for ci in range(D_MODEL // FFN_DOWN_TN):
        cols = slice(ci * FFN_DOWN_TN, (ci + 1) * FFN_DOWN_TN)
        h2_ref[:, cols] += jnp.dot(act, wd_ref[:, cols], preferred_element_type=F32)

    @pl.when(f == pl.num_programs(1) - 1)
    def _():
        xn_ref[...] = _rmsnorm_rows(h2_ref[...], gn_ref[...]).astype(xn_ref.dtype)


def _ffn(hn, h1, w_gu, w_down, g_next, out_dtype, tm=512):
    T = h1.shape[0]
    nf = D_FF // FFN_TF
    row = pl.BlockSpec((tm, D_MODEL), lambda i, f: (i, 0))
    return pl.pallas_call(
        _ffn_kernel, grid=(T // tm, nf),
        in_specs=[row, row,
                  pl.BlockSpec((D_MODEL, FFN_TF), lambda i, f: (0, f)),
                  pl.BlockSpec((D_MODEL, FFN_TF), lambda i, f: (0, nf + f)),
                  pl.BlockSpec((FFN_TF, D_MODEL), lambda i, f: (f, 0)),
                  pl.BlockSpec((1, D_MODEL), lambda i, f: (0, 0))],
        out_specs=[row, row],
        out_shape=[jax.ShapeDtypeStruct((T, D_MODEL), F32),
                   jax.ShapeDtypeStruct((T, D_MODEL), out_dtype)],
        compiler_params=_params(2), name="swiglu_ffn")(hn, h1, w_gu, w_gu, w_down, g_next)


def kernel(x, positions, norm1_g, w_in, b_in, sinks, sgu_ln_g, sgu_ln_b, sgu_w, sgu_b,
           w_attn_branch, w_sgu_branch, w_out, norm2_g, w_gate_up, w_down, final_g):
    B, S, D = x.shape
    T = B * S
    depth = w_in.shape[0]
    assert D == D_MODEL and S % ATT_TQ == 0 and w_in.shape[-1] == IN_WIDTH

    inv_freq = ROPE_THETA ** (-jnp.arange(0, ROPE_DIM, 2, dtype=F32) / ROPE_DIM)
    lane = jnp.arange(LANES) % HEAD_DIM
    invf_lanes = jnp.where(lane < ROPE_DIM, inv_freq[lane % ROPE_HALF], 0.0)[None, :]
    tabs = _rope_tables(positions.reshape(T, 1), invf_lanes)

    h = x.reshape(T, D)
    xn = _rmsnorm(h, norm1_g[0][None, :])
    for l in range(depth):
        seg = lambda lo, hi: (w_in[l, :, lo:hi].astype(BF16), b_in[l, lo:hi][None, :])
        gates = _proj(_gates_kernel, "proj_gates", xn, *seg(OFF_G, IN_WIDTH), tn=1024)
        z = _proj(_gelu_kernel, "proj_gelu", xn, *seg(OFF_Z, OFF_G), tn=1024)
        q = _proj(_q_kernel, "proj_q", xn, *seg(OFF_Q, OFF_K), tn=1024, tabs=tabs)
        kv = _proj(_kv_kernel, "proj_kv", xn, *seg(OFF_K, OFF_Z), tn=512, out_scale=2, tabs=tabs)
        y_attn = _attention(q, kv, sinks[l], S)
        y_sgu = _spatial_gating(z, sgu_ln_g[l][None, :], sgu_ln_b[l][None, :],
                                sgu_w[l].astype(BF16), sgu_b[l][..., None])
        h1, hn = _merge(y_attn, y_sgu, gates, h, w_attn_branch[l].astype(BF16),
                        w_sgu_branch[l].astype(BF16), w_out[l].astype(BF16), norm2_g[l][None, :])
        last = l == depth - 1
        g_next = final_g if last else norm1_g[l + 1]
        h, xn = _ffn(hn, h1, w_gate_up[l].astype(BF16), w_down[l].astype(BF16),
                     g_next[None, :], F32 if last else BF16)
    return xn.reshape(B, S, D)
```

```python
import functools

import jax
import jax.numpy as jnp
from jax import lax
from jax.experimental import pallas as pl
from jax.experimental.pallas import tpu as pltpu

D_MODEL = 2048
N_Q_HEADS = 16
N_KV_HEADS = 4
HEAD_DIM = 64
Q_PER_KV = N_Q_HEADS // N_KV_HEADS
ATTN_WIDTH = N_Q_HEADS * HEAD_DIM
KV_WIDTH = N_KV_HEADS * HEAD_DIM
WINDOW = 128
ROPE_THETA = 500000.0
ROPE_DIM = HEAD_DIM // 4
ROPE_HALF = ROPE_DIM // 2
SGU_WIDTH = D_MODEL // 2
SGU_GROUPS = 8
SGU_GROUP_DIM = SGU_WIDTH // SGU_GROUPS
CHUNK = 128
D_FF = 5632
EPS = 1e-5
NEG = -1e30

OFF_Q = 0
OFF_K = OFF_Q + ATTN_WIDTH
OFF_Z = OFF_K + 2 * KV_WIDTH
OFF_G = OFF_Z + 2 * SGU_WIDTH
IN_WIDTH = OFF_G + 2 * D_MODEL

LANES = 128
VMEM_LIMIT = 52 * 1024 * 1024

BF16 = jnp.bfloat16
F32 = jnp.float32


def _params(n_axes):
    return pltpu.CompilerParams(dimension_semantics=("arbitrary",) * n_axes,
                                vmem_limit_bytes=VMEM_LIMIT)


def _rmsnorm_rows(x, g):
    return x * lax.rsqrt(jnp.mean(x * x, axis=-1, keepdims=True) + EPS) * g


def _low_half_lanes(shape=(1, LANES)):
    return lax.broadcasted_iota(jnp.int32, shape, len(shape) - 1) % LANES < HEAD_DIM


def _rope_table_kernel(pos_ref, invf_ref, c_ref, s1_ref, s2_ref):
    lane = lax.broadcasted_iota(jnp.int32, c_ref.shape, 1) % HEAD_DIM
    ang = pos_ref[...].astype(F32) * invf_ref[...]
    cos, sin = jnp.cos(ang), jnp.sin(ang)
    first, second = lane < ROPE_HALF, (lane >= ROPE_HALF) & (lane < ROPE_DIM)
    c_ref[...] = jnp.where(lane < ROPE_DIM, cos, 1.0)
    s1_ref[...] = jnp.where(first, -sin, 0.0)
    s2_ref[...] = jnp.where(second, sin, 0.0)


def _rope_tables(pos, invf_lanes, tm=1024):
    T = pos.shape[0]
    tab = jax.ShapeDtypeStruct((T, LANES), F32)
    spec = pl.BlockSpec((tm, LANES), lambda i: (i, 0))
    return pl.pallas_call(
        _rope_table_kernel, grid=(T // tm,),
        in_specs=[pl.BlockSpec((tm, 1), lambda i: (i, 0)),
                  pl.BlockSpec((1, LANES), lambda i: (0, 0))],
        out_specs=[spec, spec, spec], out_shape=[tab, tab, tab],
        compiler_params=_params(1), name="rope_tables")(pos, invf_lanes)


def _norm_kernel(x_ref, g_ref, o_ref):
    o_ref[...] = _rmsnorm_rows(x_ref[...], g_ref[...]).astype(o_ref.dtype)


def _rmsnorm(x, g, tm=512):
    T, D = x.shape
    return pl.pallas_call(
        _norm_kernel, grid=(T // tm,),
        in_specs=[pl.BlockSpec((tm, D), lambda i: (i, 0)),
                  pl.BlockSpec((1, D), lambda i: (0, 0))],
        out_specs=pl.BlockSpec((tm, D), lambda i: (i, 0)),
        out_shape=jax.ShapeDtypeStruct((T, D), BF16),
        compiler_params=_params(1), name="rmsnorm0")(x, g)


PROJ_TM = 1024
PROJ_SUB = 256


def _rope(r, c, s1, s2):
    return (r * c + pltpu.roll(r, LANES - ROPE_HALF, 1) * s1
            + pltpu.roll(r, ROPE_HALF, 1) * s2)


def _sub_tiles(xn_ref, w_ref, b_ref):
    xn = xn_ref[...]
    for si in range(w_ref.shape[1] // PROJ_SUB):
        sl = slice(si * PROJ_SUB, (si + 1) * PROJ_SUB)
        yield si, jnp.dot(xn, w_ref[:, sl], preferred_element_type=F32) + b_ref[:, sl]


def _gates_kernel(xn_ref, w_ref, b_ref, o_ref):
    for si, r in _sub_tiles(xn_ref, w_ref, b_ref):
        o_ref[:, si * PROJ_SUB:(si + 1) * PROJ_SUB] = (1.0 / (1.0 + jnp.exp(-r))).astype(o_ref.dtype)


def _gelu_kernel(xn_ref, w_ref, b_ref, o_ref):
    for si, r in _sub_tiles(xn_ref, w_ref, b_ref):
        o_ref[:, si * PROJ_SUB:(si + 1) * PROJ_SUB] = (
            0.5 * r * (1.0 + lax.erf(r * (2.0 ** -0.5)))).astype(o_ref.dtype)


def _q_kernel(xn_ref, w_ref, b_ref, c_ref, s1_ref, s2_ref, o_ref):
    c, s1, s2 = (t[...] * (HEAD_DIM ** -0.5) for t in (c_ref, s1_ref, s2_ref))
    for si, r in _sub_tiles(xn_ref, w_ref, b_ref):
        for gi in range(PROJ_SUB // LANES):
            x = r[:, gi * LANES:(gi + 1) * LANES]
            lo = si * PROJ_SUB + gi * LANES
            o_ref[:, lo:lo + LANES] = _rope(x, c, s1, s2).astype(o_ref.dtype)


def _kv_kernel(xn_ref, w_ref, b_ref, c_ref, s1_ref, s2_ref, o_ref):
    c, s1, s2 = c_ref[...], s1_ref[...], s2_ref[...]
    low = _low_half_lanes()
    for si, r in _sub_tiles(xn_ref, w_ref, b_ref):
        for gi in range(PROJ_SUB // LANES):
            x = r[:, gi * LANES:(gi + 1) * LANES]
            if si * PROJ_SUB < KV_WIDTH:
                x = _rope(x, c, s1, s2)
            swapped = pltpu.roll(x, HEAD_DIM, 1)
            lo = 2 * (si * PROJ_SUB + gi * LANES)
            o_ref[:, lo:lo + LANES] = jnp.where(low, x, swapped).astype(o_ref.dtype)
            o_ref[:, lo + LANES:lo + 2 * LANES] = jnp.where(low, swapped, x).astype(o_ref.dtype)


def _proj(body, name, xn, w, b, layer, tn, out_scale=1, tabs=()):
    T, tm = xn.shape[0], PROJ_TM
    width = w.shape[-1]
    tab_spec = pl.BlockSpec((tm, LANES), lambda i, j: (i, 0))
    return pl.pallas_call(
        body, grid=(T // tm, width // tn),
        in_specs=[pl.BlockSpec((tm, D_MODEL), lambda i, j: (i, 0)),
                  pl.BlockSpec((None, D_MODEL, tn), lambda i, j: (layer, 0, j)),
                  pl.BlockSpec((None, 1, tn), lambda i, j: (layer, 0, j))] + [tab_spec] * len(tabs),
        out_specs=pl.BlockSpec((tm, out_scale * tn), lambda i, j: (i, j)),
        out_shape=jax.ShapeDtypeStruct((T, out_scale * width), BF16),
        compiler_params=_params(2), name=name)(xn, w, b, *tabs)


ATT_TQ = 512
ATT_SUB = ATT_TQ // WINDOW
KVDUP_WIDTH = 2 * KV_WIDTH


def _attn_kernel(blocks_per_seq, sinks_ref, q_ref, kc_ref, vc_ref, kp_ref, vp_ref, o_ref):
    t = pl.program_id(0)
    qi = lax.broadcasted_iota(jnp.int32, (WINDOW, 2 * WINDOW), 0)
    kj = lax.broadcasted_iota(jnp.int32, (WINDOW, 2 * WINDOW), 1)
    rel = qi + WINDOW - kj
    band_ok = (rel >= 0) & (rel < WINDOW)
    low = _low_half_lanes()
    ones_low = jnp.broadcast_to(jnp.where(low, 1.0, 0.0).astype(BF16), (2 * WINDOW, LANES))
    ones_high = jnp.broadcast_to(jnp.where(low, 0.0, 1.0).astype(BF16), (2 * WINDOW, LANES))
    for c in range(ATT_SUB):
        rows = slice(c * WINDOW, (c + 1) * WINDOW)
        if c == 0:
            first_key = jnp.where((t * ATT_SUB) % blocks_per_seq == 0, WINDOW, 0)
            mask = band_ok & (kj >= first_key)
            k_prev, v_prev = kp_ref[...], vp_ref[...]
        else:
            mask = band_ok
            prev_rows = slice((c - 1) * WINDOW, c * WINDOW)
            k_prev, v_prev = kc_ref[prev_rows, :], vc_ref[prev_rows, :]
        k_band = jnp.concatenate([k_prev, kc_ref[rows, :]], axis=0)
        v_band = jnp.concatenate([v_prev, vc_ref[rows, :]], axis=0)
        for g in range(N_KV_HEADS):
            lanes_g = slice(g * LANES, (g + 1) * LANES)
            kg, vg = k_band[:, lanes_g], v_band[:, lanes_g]
            zero = jnp.zeros_like(vg)
            rhs_pv = jnp.concatenate(
                [jnp.concatenate([jnp.where(low, vg, zero), ones_low], axis=1),
                 jnp.concatenate([jnp.where(low, zero, vg), ones_high], axis=1)], axis=0)
            q_pairs = [q_ref[rows, (2 * g + i) * LANES:(2 * g + i + 1) * LANES] for i in range(2)]
            zq = jnp.zeros_like(q_pairs[0])
            lhs = jnp.concatenate([jnp.where(low, q_pairs[0], zq), jnp.where(low, zq, q_pairs[0]),
                                   jnp.where(low, q_pairs[1], zq), jnp.where(low, zq, q_pairs[1])],
                                  axis=0)
            s = lax.dot_general(lhs, kg, (((1,), (1,)), ((), ())), preferred_element_type=F32)
            probs, sink_terms = [], []
            for i in range(Q_PER_KV):
                sink = sinks_ref[g * Q_PER_KV + i]
                si = jnp.where(mask, s[i * WINDOW:(i + 1) * WINDOW, :], NEG)
                m = jnp.maximum(jnp.max(si, axis=-1, keepdims=True), sink)
                probs.append(jnp.exp(si - m).astype(BF16))
                sink_terms.append(jnp.exp(sink - m))
            for pair in range(2):
                p2 = jnp.concatenate(probs[2 * pair:2 * pair + 2], axis=1)
                r = jnp.dot(p2, rhs_pv, preferred_element_type=F32)
                den = r[:, LANES:] + jnp.where(low, sink_terms[2 * pair], sink_terms[2 * pair + 1])
                out_lanes = slice((2 * g + pair) * LANES, (2 * g + pair + 1) * LANES)
                o_ref[rows, out_lanes] = (r[:, :LANES] / den).astype(o_ref.dtype)


def _attention(q, kv, sinks, seq_len):
    T = q.shape[0]
    prev = lambda t: jnp.maximum(t * ATT_SUB - 1, 0)
    return pl.pallas_call(
        functools.partial(_attn_kernel, seq_len // WINDOW), grid=(T // ATT_TQ,),
        in_specs=[pl.BlockSpec(memory_space=pltpu.SMEM),
                  pl.BlockSpec((ATT_TQ, ATTN_WIDTH), lambda t: (t, 0)),
                  pl.BlockSpec((ATT_TQ, KVDUP_WIDTH), lambda t: (t, 0)),
                  pl.BlockSpec((ATT_TQ, KVDUP_WIDTH), lambda t: (t, 1)),
                  pl.BlockSpec((WINDOW, KVDUP_WIDTH), lambda t: (prev(t), 0)),
                  pl.BlockSpec((WINDOW, KVDUP_WIDTH), lambda t: (prev(t), 1))],
        out_specs=pl.BlockSpec((ATT_TQ, ATTN_WIDTH), lambda t: (t, 0)),
        out_shape=jax.ShapeDtypeStruct((T, ATTN_WIDTH), BF16),
        compiler_params=_params(1), name="swa_attention")(sinks, q, kv, kv, kv, kv)


SGU_TQ = 512


def _sgu_kernel(u_ref, v_ref, lng_ref, lnb_ref, w_ref, b_ref, o_ref):
    v = v_ref[...].astype(F32)
    mu = jnp.mean(v, axis=-1, keepdims=True)
    d = v - mu
    var = jnp.mean(d * d, axis=-1, keepdims=True)
    vn = (d * lax.rsqrt(var + EPS) * lng_ref[...] + lnb_ref[...]).astype(BF16)
    ti = lax.broadcasted_iota(jnp.int32, (CHUNK, CHUNK), 0)
    si = lax.broadcasted_iota(jnp.int32, (CHUNK, CHUNK), 1)
    causal = si <= ti
    for g in range(SGU_GROUPS):
        cols = slice(g * SGU_GROUP_DIM, (g + 1) * SGU_GROUP_DIM)
        w = jnp.where(causal, w_ref[g], jnp.zeros_like(w_ref[g]))
        bias = b_ref[g]
        for c in range(SGU_TQ // CHUNK):
            rows = slice(c * CHUNK, (c + 1) * CHUNK)
            sv = jnp.dot(w, vn[rows, cols], preferred_element_type=F32) + bias
            o_ref[rows, cols] = (u_ref[rows, cols].astype(F32) * sv).astype(o_ref.dtype)


def _spatial_gating(z, ln_g, ln_b, w, b):
    T = z.shape[0]
    full = lambda shape: pl.BlockSpec(shape, lambda t: (0,) * len(shape))
    return pl.pallas_call(
        _sgu_kernel, grid=(T // SGU_TQ,),
        in_specs=[pl.BlockSpec((SGU_TQ, SGU_WIDTH), lambda t: (t, 0)),
                  pl.BlockSpec((SGU_TQ, SGU_WIDTH), lambda t: (t, 1)),
                  full((1, SGU_WIDTH)), full((1, SGU_WIDTH)),
                  full((SGU_GROUPS, CHUNK, CHUNK)), full((SGU_GROUPS, CHUNK, 1))],
        out_specs=pl.BlockSpec((SGU_TQ, SGU_WIDTH), lambda t: (t, 0)),
        out_shape=jax.ShapeDtypeStruct((T, SGU_WIDTH), BF16),
        compiler_params=_params(1), name="spatial_gating")(z, z, ln_g, ln_b, w, b)


MRG_TN = 512


def _merge_kernel(ya_ref, ys_ref, g_ref, h_ref, wa_ref, ws_ref, wo_ref, g2_ref,
                  h1_ref, hn_ref, merged_ref):
    ya, ys = ya_ref[...], ys_ref[...]
    for j in range(D_MODEL // MRG_TN):
        cols = slice(j * MRG_TN, (j + 1) * MRG_TN)
        gs_cols = slice(D_MODEL + j * MRG_TN, D_MODEL + (j + 1) * MRG_TN)
        a = jnp.dot(ya, wa_ref[:, cols], preferred_element_type=F32)
        s = jnp.dot(ys, ws_ref[:, cols], preferred_element_type=F32)
        merged_ref[:, cols] = (g_ref[:, cols].astype(F32) * a
                               + g_ref[:, gs_cols].astype(F32) * s).astype(BF16)
    merged = merged_ref[...]
    sum_sq = jnp.zeros((merged.shape[0], 1), F32)
    for j in range(D_MODEL // MRG_TN):
        cols = slice(j * MRG_TN, (j + 1) * MRG_TN)
        h1 = h_ref[:, cols] + jnp.dot(merged, wo_ref[:, cols], preferred_element_type=F32)
        h1_ref[:, cols] = h1
        sum_sq += jnp.sum(h1 * h1, axis=-1, keepdims=True)
    scale = lax.rsqrt(sum_sq * (1.0 / D_MODEL) + EPS)
    hn_ref[...] = (h1_ref[...] * scale * g2_ref[...]).astype(hn_ref.dtype)


def _merge(y_attn, y_sgu, gates, h, w_ab, w_sb, w_out, g2, layer, tm=256):
    T = h.shape[0]
    row = lambda width: pl.BlockSpec((tm, width), lambda i: (i, 0))
    resident = lambda shape: pl.BlockSpec((None,) + shape, lambda i: (layer, 0, 0),
                                          pipeline_mode=pl.Buffered(1))
    return pl.pallas_call(
        _merge_kernel, grid=(T // tm,),
        in_specs=[row(ATTN_WIDTH), row(SGU_WIDTH), row(2 * D_MODEL), row(D_MODEL),
                  resident((ATTN_WIDTH, D_MODEL)), resident((SGU_WIDTH, D_MODEL)),
                  resident((D_MODEL, D_MODEL)), resident((1, D_MODEL))],
        out_specs=[row(D_MODEL), row(D_MODEL)],
        out_shape=[jax.ShapeDtypeStruct((T, D_MODEL), F32),
                   jax.ShapeDtypeStruct((T, D_MODEL), BF16)],
        scratch_shapes=[pltpu.VMEM((tm, D_MODEL), BF16)],
        compiler_params=_params(1), name="merge_out_proj")(
            y_attn, y_sgu, gates, h, w_ab, w_sb, w_out, g2)


FFN_TF = 512
FFN_SUB = 256
FFN_DOWN_TN = 512


def _ffn_kernel(hn_ref, h1_ref, wg_ref, wu_ref, wd_ref, gn_ref, *rest):
    convert_next = len(rest) > 2
    if convert_next:
        next_gu_ref, next_dn_ref, h2_ref, xn_ref, next_gu_bf_ref, next_dn_bf_ref = rest
    else:
        h2_ref, xn_ref = rest
    f = pl.program_id(1)

    @pl.when(f == 0)
    def _():
        h2_ref[...] = h1_ref[...]

    if convert_next:
        next_gu_bf_ref[...] = next_gu_ref[...].astype(BF16)
        next_dn_bf_ref[...] = next_dn_ref[...].astype(BF16)

    hn = hn_ref[...]
    acts = []
    for si in range(FFN_TF // FFN_SUB):
        sl = slice(si * FFN_SUB, (si + 1) * FFN_SUB)
        gate = jnp.dot(hn, wg_ref[:, sl], preferred_element_type=F32)
        up = jnp.dot(hn, wu_ref[:, sl], preferred_element_type=F32)
        acts.append((gate * (1.0 / (1.0 + jnp.exp(-gate))) * up).astype(BF16))
    act = jnp.concatenate(acts, axis=1)
    for ci in range(D_MODEL // FFN_DOWN_TN):
        cols = slice(ci * FFN_DOWN_TN, (ci + 1) * FFN_DOWN_TN)
        h2_ref[:, cols] += jnp.dot(act, wd_ref[:, cols], preferred_element_type=F32)

    @pl.when(f == pl.num_programs(1) - 1)
    def _():
        xn_ref[...] = _rmsnorm_rows(h2_ref[...], gn_ref[...]).astype(xn_ref.dtype)


def _ffn(hn, h1, w_gu, w_down, g_next, out_dtype, next_f32=None, tm=512):
    T = h1.shape[0]
    n_rows, nf = T // tm, D_FF // FFN_TF
    row = pl.BlockSpec((tm, D_MODEL), lambda i, f: (i, 0))
    in_specs = [row, row,
                pl.BlockSpec((None, D_MODEL, FFN_TF), lambda i, f: (0, 0, f)),
                pl.BlockSpec((None, D_MODEL, FFN_TF), lambda i, f: (0, 0, nf + f)),
                pl.BlockSpec((None, FFN_TF, D_MODEL), lambda i, f: (0, f, 0)),
                pl.BlockSpec((1, D_MODEL), lambda i, f: (0, 0))]
    out_specs = [row, row]
    out_shape = [jax.ShapeDtypeStruct((T, D_MODEL), F32), jax.ShapeDtypeStruct((T, D_MODEL), out_dtype)]
    args = [hn, h1, w_gu, w_gu, w_down, g_next]
    if next_f32 is not None:
        gu32, dn32, nl = next_f32
        gu_blk = (None, D_MODEL // n_rows, 2 * D_FF // nf)
        dn_blk = (None, D_FF // (n_rows * nf), D_MODEL)
        in_specs += [pl.BlockSpec(gu_blk, lambda i, f: (nl, i, f)),
                     pl.BlockSpec(dn_blk, lambda i, f: (nl, i * nf + f, 0))]
        out_specs += [pl.BlockSpec(gu_blk, lambda i, f: (0, i, f)),
                      pl.BlockSpec(dn_blk, lambda i, f: (0, i * nf + f, 0))]
        out_shape += [jax.ShapeDtypeStruct((1,) + gu32.shape[1:], BF16),
                      jax.ShapeDtypeStruct((1,) + dn32.shape[1:], BF16)]
        args += [gu32, dn32]
    return pl.pallas_call(
        _ffn_kernel, grid=(n_rows, nf), in_specs=in_specs, out_specs=out_specs, out_shape=out_shape,
        compiler_params=_params(2), name="swiglu_ffn")(*args)


def kernel(x, positions, norm1_g, w_in, b_in, sinks, sgu_ln_g, sgu_ln_b, sgu_w, sgu_b,
           w_attn_branch, w_sgu_branch, w_out, norm2_g, w_gate_up, w_down, final_g):
    B, S, D = x.shape
    T = B * S
    depth = w_in.shape[0]
    assert D == D_MODEL and S % ATT_TQ == 0 and w_in.shape[-1] == IN_WIDTH

    inv_freq = ROPE_THETA ** (-jnp.arange(0, ROPE_DIM, 2, dtype=F32) / ROPE_DIM)
    lane = jnp.arange(LANES) % HEAD_DIM
    invf_lanes = jnp.where(lane < ROPE_DIM, inv_freq[lane % ROPE_HALF], 0.0)[None, :]
    tabs = _rope_tables(positions.reshape(T, 1), invf_lanes)

    seg = lambda lo, hi: (w_in[:, :, lo:hi].astype(BF16), b_in[:, None, lo:hi])
    seg_g, seg_z, seg_q, seg_kv = seg(OFF_G, IN_WIDTH), seg(OFF_Z, OFF_G), seg(OFF_Q, OFF_K), seg(OFF_K, OFF_Z)
    w_ab, w_sb, w_o = (w.astype(BF16) for w in (w_attn_branch, w_sgu_branch, w_out))
    w_gu, w_dn = w_gate_up[:1].astype(BF16), w_down[:1].astype(BF16)
    sgu_w_b, sgu_b_c, g2 = sgu_w.astype(BF16), sgu_b[..., None], norm2_g[:, None, :]

    h = x.reshape(T, D)
    xn = _rmsnorm(h, norm1_g[0][None, :])
    for l in range(depth):
        gates = _proj(_gates_kernel, "proj_gates", xn, *seg_g, l, tn=1024)
        z = _proj(_gelu_kernel, "proj_gelu", xn, *seg_z, l, tn=1024)
        q = _proj(_q_kernel, "proj_q", xn, *seg_q, l, tn=1024, tabs=tabs)
        kv = _proj(_kv_kernel, "proj_kv", xn, *seg_kv, l, tn=512, out_scale=2, tabs=tabs)
        y_attn = _attention(q, kv, sinks[l], S)
        y_sgu = _spatial_gating(z, sgu_ln_g[l][None, :], sgu_ln_b[l][None, :], sgu_w_b[l], sgu_b_c[l])
        h1, hn = _merge(y_attn, y_sgu, gates, h, w_ab, w_sb, w_o, g2, l)
        if l == depth - 1:
            _, out = _ffn(hn, h1, w_gu, w_dn, final_g[None, :], F32)
        else:
            h, xn, w_gu, w_dn = _ffn(hn, h1, w_gu, w_dn, norm1_g[l + 1][None, :], BF16,
                                     next_f32=(w_gate_up, w_down, l + 1))
    return out.reshape(B, S, D)
```

```python
import functools

import jax
import jax.numpy as jnp
from jax import lax
from jax.experimental import pallas as pl
from jax.experimental.pallas import tpu as pltpu

D_MODEL = 2048
N_Q_HEADS = 16
N_KV_HEADS = 4
HEAD_DIM = 64
Q_PER_KV = N_Q_HEADS // N_KV_HEADS
ATTN_WIDTH = N_Q_HEADS * HEAD_DIM
KV_WIDTH = N_KV_HEADS * HEAD_DIM
WINDOW = 128
ROPE_THETA = 500000.0
ROPE_DIM = HEAD_DIM // 4
ROPE_HALF = ROPE_DIM // 2
SGU_WIDTH = D_MODEL // 2
SGU_GROUPS = 8
SGU_GROUP_DIM = SGU_WIDTH // SGU_GROUPS
CHUNK = 128
D_FF = 5632
EPS = 1e-5
NEG = -1e30

OFF_Q = 0
OFF_K = OFF_Q + ATTN_WIDTH
OFF_Z = OFF_K + 2 * KV_WIDTH
OFF_G = OFF_Z + 2 * SGU_WIDTH
IN_WIDTH = OFF_G + 2 * D_MODEL

LANES = 128
VMEM_LIMIT = 52 * 1024 * 1024

BF16 = jnp.bfloat16
F32 = jnp.float32


def _params(n_axes):
    return pltpu.CompilerParams(dimension_semantics=("arbitrary",) * n_axes,
                                vmem_limit_bytes=VMEM_LIMIT)


def _rmsnorm_rows(x, g):
    return x * lax.rsqrt(jnp.mean(x * x, axis=-1, keepdims=True) + EPS) * g


def _low_half_lanes(shape=(1, LANES)):
    return lax.broadcasted_iota(jnp.int32, shape, len(shape) - 1) % LANES < HEAD_DIM


def _rope_table_kernel(pos_ref, invf_ref, c_ref, s1_ref, s2_ref):
    lane = lax.broadcasted_iota(jnp.int32, c_ref.shape, 1) % HEAD_DIM
    ang = pos_ref[...].astype(F32) * invf_ref[...]
    cos, sin = jnp.cos(ang), jnp.sin(ang)
    first, second = lane < ROPE_HALF, (lane >= ROPE_HALF) & (lane < ROPE_DIM)
    c_ref[...] = jnp.where(lane < ROPE_DIM, cos, 1.0)
    s1_ref[...] = jnp.where(first, -sin, 0.0)
    s2_ref[...] = jnp.where(second, sin, 0.0)


def _rope_tables(pos, invf_lanes, tm=1024):
    T = pos.shape[0]
    tab = jax.ShapeDtypeStruct((T, LANES), F32)
    spec = pl.BlockSpec((tm, LANES), lambda i: (i, 0))
    return pl.pallas_call(
        _rope_table_kernel, grid=(T // tm,),
        in_specs=[pl.BlockSpec((tm, 1), lambda i: (i, 0)),
                  pl.BlockSpec((1, LANES), lambda i: (0, 0))],
        out_specs=[spec, spec, spec], out_shape=[tab, tab, tab],
        compiler_params=_params(1), name="rope_tables")(pos, invf_lanes)


def _norm_kernel(x_ref, g_ref, o_ref):
    o_ref[...] = _rmsnorm_rows(x_ref[...], g_ref[...]).astype(o_ref.dtype)


def _rmsnorm(x, g, tm=512):
    T, D = x.shape
    return pl.pallas_call(
        _norm_kernel, grid=(T // tm,),
        in_specs=[pl.BlockSpec((tm, D), lambda i: (i, 0)),
                  pl.BlockSpec((1, D), lambda i: (0, 0))],
        out_specs=pl.BlockSpec((tm, D), lambda i: (i, 0)),
        out_shape=jax.ShapeDtypeStruct((T, D), BF16),
        compiler_params=_params(1), name="rmsnorm0")(x, g)


PROJ_TM = 1024
PROJ_SUB = 256


def _rope(r, c, s1, s2):
    return (r * c + pltpu.roll(r, LANES - ROPE_HALF, 1) * s1
            + pltpu.roll(r, ROPE_HALF, 1) * s2)


def _sub_tiles(xn_ref, w_ref, b_ref):
    xn = xn_ref[...]
    for si in range(w_ref.shape[1] // PROJ_SUB):
        sl = slice(si * PROJ_SUB, (si + 1) * PROJ_SUB)
        yield si, jnp.dot(xn, w_ref[:, sl], preferred_element_type=F32) + b_ref[:, sl]


def _gates_kernel(xn_ref, w_ref, b_ref, gu32_ref, dn32_ref, o_ref, gu_bf_ref, dn_bf_ref):
    gu_bf_ref[...] = gu32_ref[...].astype(BF16)
    dn_bf_ref[...] = dn32_ref[...].astype(BF16)
    for si, r in _sub_tiles(xn_ref, w_ref, b_ref):
        o_ref[:, si * PROJ_SUB:(si + 1) * PROJ_SUB] = (1.0 / (1.0 + jnp.exp(-r))).astype(o_ref.dtype)


def _gelu_kernel(xn_ref, w_ref, b_ref, o_ref):
    for si, r in _sub_tiles(xn_ref, w_ref, b_ref):
        o_ref[:, si * PROJ_SUB:(si + 1) * PROJ_SUB] = (
            0.5 * r * (1.0 + lax.erf(r * (2.0 ** -0.5)))).astype(o_ref.dtype)


def _q_kernel(xn_ref, w_ref, b_ref, c_ref, s1_ref, s2_ref, o_ref):
    c, s1, s2 = (t[...] * (HEAD_DIM ** -0.5) for t in (c_ref, s1_ref, s2_ref))
    for si, r in _sub_tiles(xn_ref, w_ref, b_ref):
        for gi in range(PROJ_SUB // LANES):
            x = r[:, gi * LANES:(gi + 1) * LANES]
            lo = si * PROJ_SUB + gi * LANES
            o_ref[:, lo:lo + LANES] = _rope(x, c, s1, s2).astype(o_ref.dtype)


def _kv_kernel(xn_ref, w_ref, b_ref, c_ref, s1_ref, s2_ref, o_ref):
    c, s1, s2 = c_ref[...], s1_ref[...], s2_ref[...]
    low = _low_half_lanes()
    for si, r in _sub_tiles(xn_ref, w_ref, b_ref):
        for gi in range(PROJ_SUB // LANES):
            x = r[:, gi * LANES:(gi + 1) * LANES]
            if si * PROJ_SUB < KV_WIDTH:
                x = _rope(x, c, s1, s2)
            swapped = pltpu.roll(x, HEAD_DIM, 1)
            lo = 2 * (si * PROJ_SUB + gi * LANES)
            o_ref[:, lo:lo + LANES] = jnp.where(low, x, swapped).astype(o_ref.dtype)
            o_ref[:, lo + LANES:lo + 2 * LANES] = jnp.where(low, swapped, x).astype(o_ref.dtype)


def _proj(body, name, xn, w, b, tn, out_scale=1, tabs=(), cast=None):
    T, tm = xn.shape[0], PROJ_TM
    width = w.shape[-1]
    n_i, n_j = T // tm, width // tn
    tab_spec = pl.BlockSpec((tm, LANES), lambda i, j: (i, 0))
    in_specs = [pl.BlockSpec((tm, D_MODEL), lambda i, j: (i, 0)),
                pl.BlockSpec((None, D_MODEL, tn), lambda i, j: (0, 0, j)),
                pl.BlockSpec((None, 1, tn), lambda i, j: (0, 0, j))] + [tab_spec] * len(tabs)
    out_specs = [pl.BlockSpec((tm, out_scale * tn), lambda i, j: (i, j))]
    out_shape = [jax.ShapeDtypeStruct((T, out_scale * width), BF16)]
    args = [xn, w, b, *tabs]
    if cast is not None:
        arrays, layer = cast
        for a in arrays:
            blk = (None, a.shape[1] // n_i, a.shape[2] // n_j)
            in_specs.append(pl.BlockSpec(blk, lambda i, j: (layer, i, j)))
            out_specs.append(pl.BlockSpec(blk, lambda i, j: (0, i, j)))
            out_shape.append(jax.ShapeDtypeStruct((1,) + a.shape[1:], BF16))
            args.append(a)
    out = pl.pallas_call(
        body, grid=(n_i, n_j), in_specs=in_specs, out_specs=out_specs, out_shape=out_shape,
        compiler_params=_params(2), name=name)(*args)
    return out[0] if cast is None else out


ATT_TQ = 512
ATT_SUB = ATT_TQ // WINDOW
KVDUP_WIDTH = 2 * KV_WIDTH


def _attn_kernel(blocks_per_seq, sinks_ref, q_ref, kc_ref, vc_ref, kp_ref, vp_ref, o_ref):
    t = pl.program_id(0)
    qi = lax.broadcasted_iota(jnp.int32, (WINDOW, 2 * WINDOW), 0)
    kj = lax.broadcasted_iota(jnp.int32, (WINDOW, 2 * WINDOW), 1)
    rel = qi + WINDOW - kj
    band_ok = (rel >= 0) & (rel < WINDOW)
    low = _low_half_lanes()
    ones_low = jnp.broadcast_to(jnp.where(low, 1.0, 0.0).astype(BF16), (2 * WINDOW, LANES))
    ones_high = jnp.broadcast_to(jnp.where(low, 0.0, 1.0).astype(BF16), (2 * WINDOW, LANES))
    for c in range(ATT_SUB):
        rows = slice(c * WINDOW, (c + 1) * WINDOW)
        if c == 0:
            first_key = jnp.where((t * ATT_SUB) % blocks_per_seq == 0, WINDOW, 0)
            mask = band_ok & (kj >= first_key)
            k_prev, v_prev = kp_ref[...], vp_ref[...]
        else:
            mask = band_ok
            prev_rows = slice((c - 1) * WINDOW, c * WINDOW)
            k_prev, v_prev = kc_ref[prev_rows, :], vc_ref[prev_rows, :]
        k_band = jnp.concatenate([k_prev, kc_ref[rows, :]], axis=0)
        v_band = jnp.concatenate([v_prev, vc_ref[rows, :]], axis=0)
        for g in range(N_KV_HEADS):
            lanes_g = slice(g * LANES, (g + 1) * LANES)
            kg, vg = k_band[:, lanes_g], v_band[:, lanes_g]
            zero = jnp.zeros_like(vg)
            rhs_pv = jnp.concatenate(
                [jnp.concatenate([jnp.where(low, vg, zero), ones_low], axis=1),
                 jnp.concatenate([jnp.where(low, zero, vg), ones_high], axis=1)], axis=0)
            q_pairs = [q_ref[rows, (2 * g + i) * LANES:(2 * g + i + 1) * LANES] for i in range(2)]
            zq = jnp.zeros_like(q_pairs[0])
            lhs = jnp.concatenate([jnp.where(low, q_pairs[0], zq), jnp.where(low, zq, q_pairs[0]),
                                   jnp.where(low, q_pairs[1], zq), jnp.where(low, zq, q_pairs[1])],
                                  axis=0)
            s = lax.dot_general(lhs, kg, (((1,), (1,)), ((), ())), preferred_element_type=F32)
            probs, sink_terms = [], []
            for i in range(Q_PER_KV):
                sink = sinks_ref[g * Q_PER_KV + i]
                si = jnp.where(mask, s[i * WINDOW:(i + 1) * WINDOW, :], NEG)
                m = jnp.maximum(jnp.max(si, axis=-1, keepdims=True), sink)
                probs.append(jnp.exp(si - m).astype(BF16))
                sink_terms.append(jnp.exp(sink - m))
            for pair in range(2):
                p2 = jnp.concatenate(probs[2 * pair:2 * pair + 2], axis=1)
                r = jnp.dot(p2, rhs_pv, preferred_element_type=F32)
                den = r[:, LANES:] + jnp.where(low, sink_terms[2 * pair], sink_terms[2 * pair + 1])
                out_lanes = slice((2 * g + pair) * LANES, (2 * g + pair + 1) * LANES)
                o_ref[rows, out_lanes] = (r[:, :LANES] / den).astype(o_ref.dtype)


def _attention(q, kv, sinks, seq_len):
    T = q.shape[0]
    prev = lambda t: jnp.maximum(t * ATT_SUB - 1, 0)
    return pl.pallas_call(
        functools.partial(_attn_kernel, seq_len // WINDOW), grid=(T // ATT_TQ,),
        in_specs=[pl.BlockSpec(memory_space=pltpu.SMEM),
                  pl.BlockSpec((ATT_TQ, ATTN_WIDTH), lambda t: (t, 0)),
                  pl.BlockSpec((ATT_TQ, KVDUP_WIDTH), lambda t: (t, 0)),
                  pl.BlockSpec((ATT_TQ, KVDUP_WIDTH), lambda t: (t, 1)),
                  pl.BlockSpec((WINDOW, KVDUP_WIDTH), lambda t: (prev(t), 0)),
                  pl.BlockSpec((WINDOW, KVDUP_WIDTH), lambda t: (prev(t), 1))],
        out_specs=pl.BlockSpec((ATT_TQ, ATTN_WIDTH), lambda t: (t, 0)),
        out_shape=jax.ShapeDtypeStruct((T, ATTN_WIDTH), BF16),
        compiler_params=_params(1), name="swa_attention")(sinks, q, kv, kv, kv, kv)


SGU_TQ = 512


def _sgu_kernel(u_ref, v_ref, lng_ref, lnb_ref, w_ref, b_ref, o_ref):
    v = v_ref[...].astype(F32)
    mu = jnp.mean(v, axis=-1, keepdims=True)
    d = v - mu
    var = jnp.mean(d * d, axis=-1, keepdims=True)
    vn = (d * lax.rsqrt(var + EPS) * lng_ref[...] + lnb_ref[...]).astype(BF16)
    ti = lax.broadcasted_iota(jnp.int32, (CHUNK, CHUNK), 0)
    si = lax.broadcasted_iota(jnp.int32, (CHUNK, CHUNK), 1)
    causal = si <= ti
    for g in range(SGU_GROUPS):
        cols = slice(g * SGU_GROUP_DIM, (g + 1) * SGU_GROUP_DIM)
        w = jnp.where(causal, w_ref[g], jnp.zeros_like(w_ref[g]))
        bias = b_ref[g]
        for c in range(SGU_TQ // CHUNK):
            rows = slice(c * CHUNK, (c + 1) * CHUNK)
            sv = jnp.dot(w, vn[rows, cols], preferred_element_type=F32) + bias
            o_ref[rows, cols] = (u_ref[rows, cols].astype(F32) * sv).astype(o_ref.dtype)


def _spatial_gating(z, ln_g, ln_b, w, b):
    T = z.shape[0]
    full = lambda shape: pl.BlockSpec(shape, lambda t: (0,) * len(shape))
    return pl.pallas_call(
        _sgu_kernel, grid=(T // SGU_TQ,),
        in_specs=[pl.BlockSpec((SGU_TQ, SGU_WIDTH), lambda t: (t, 0)),
                  pl.BlockSpec((SGU_TQ, SGU_WIDTH), lambda t: (t, 1)),
                  full((1, SGU_WIDTH)), full((1, SGU_WIDTH)),
                  full((SGU_GROUPS, CHUNK, CHUNK)), full((SGU_GROUPS, CHUNK, 1))],
        out_specs=pl.BlockSpec((SGU_TQ, SGU_WIDTH), lambda t: (t, 0)),
        out_shape=jax.ShapeDtypeStruct((T, SGU_WIDTH), BF16),
        compiler_params=_params(1), name="spatial_gating")(z, z, ln_g, ln_b, w, b)


MRG_TN = 512


IN_SEGMENTS = ((OFF_Q, OFF_K), (OFF_K, OFF_Z), (OFF_Z, OFF_G), (OFF_G, IN_WIDTH))


def _merge_kernel(ya_ref, ys_ref, g_ref, h_ref, wa_ref, ws_ref, wo_ref, g2_ref, *rest):
    if len(rest) == 3:
        h1_ref, hn_ref, merged_ref = rest
    else:
        next_in_ref, h1_ref, hn_ref, *seg_refs, merged_ref = rest
        for (lo, hi), seg_ref in zip(IN_SEGMENTS, seg_refs):
            seg_ref[...] = next_in_ref[:, lo:hi].astype(BF16)
    ya, ys = ya_ref[...], ys_ref[...]
    for j in range(D_MODEL // MRG_TN):
        cols = slice(j * MRG_TN, (j + 1) * MRG_TN)
        gs_cols = slice(D_MODEL + j * MRG_TN, D_MODEL + (j + 1) * MRG_TN)
        a = jnp.dot(ya, wa_ref[:, cols], preferred_element_type=F32)
        s = jnp.dot(ys, ws_ref[:, cols], preferred_element_type=F32)
        merged_ref[:, cols] = (g_ref[:, cols].astype(F32) * a
                               + g_ref[:, gs_cols].astype(F32) * s).astype(BF16)
    merged = merged_ref[...]
    sum_sq = jnp.zeros((merged.shape[0], 1), F32)
    for j in range(D_MODEL // MRG_TN):
        cols = slice(j * MRG_TN, (j + 1) * MRG_TN)
        h1 = h_ref[:, cols] + jnp.dot(merged, wo_ref[:, cols], preferred_element_type=F32)
        h1_ref[:, cols] = h1
        sum_sq += jnp.sum(h1 * h1, axis=-1, keepdims=True)
    scale = lax.rsqrt(sum_sq * (1.0 / D_MODEL) + EPS)
    hn_ref[...] = (h1_ref[...] * scale * g2_ref[...]).astype(hn_ref.dtype)


def _merge(y_attn, y_sgu, gates, h, w_ab, w_sb, w_out, g2, layer, w_in32=None, tm=256):
    T = h.shape[0]
    n_rows = T // tm
    row = lambda width: pl.BlockSpec((tm, width), lambda i: (i, 0))
    resident = lambda shape: pl.BlockSpec((None,) + shape, lambda i: (layer, 0, 0),
                                          pipeline_mode=pl.Buffered(1))
    in_specs = [row(ATTN_WIDTH), row(SGU_WIDTH), row(2 * D_MODEL), row(D_MODEL),
                resident((ATTN_WIDTH, D_MODEL)), resident((SGU_WIDTH, D_MODEL)),
                resident((D_MODEL, D_MODEL)), resident((1, D_MODEL))]
    out_specs = [row(D_MODEL), row(D_MODEL)]
    out_shape = [jax.ShapeDtypeStruct((T, D_MODEL), F32), jax.ShapeDtypeStruct((T, D_MODEL), BF16)]
    args = [y_attn, y_sgu, gates, h, w_ab, w_sb, w_out, g2]
    if w_in32 is not None:
        rows = D_MODEL // n_rows
        in_specs.append(pl.BlockSpec((None, rows, IN_WIDTH), lambda i: (layer + 1, i, 0)))
        args.append(w_in32)
        for lo, hi in IN_SEGMENTS:
            out_specs.append(pl.BlockSpec((None, rows, hi - lo), lambda i: (0, i, 0)))
            out_shape.append(jax.ShapeDtypeStruct((1, D_MODEL, hi - lo), BF16))
    return pl.pallas_call(
        _merge_kernel, grid=(n_rows,), in_specs=in_specs, out_specs=out_specs, out_shape=out_shape,
        scratch_shapes=[pltpu.VMEM((tm, D_MODEL), BF16)],
        compiler_params=_params(1), name="merge_out_proj")(*args)


FFN_TM = 1024
FFN_TF = 512
FFN_SUB = 256
FFN_DOWN_TN = 512


def _ffn_kernel(last, hn_hbm, h1_hbm, wg_ref, wu_ref, wd_ref, gn_ref, *rest):
    if last:
        out_hbm, acc, act, load_sem, store_sem = rest
    else:
        h2_hbm, xn_hbm, acc, act, load_sem, store_sem = rest
    i, f = pl.program_id(0), pl.program_id(1)
    n_rows, nf = pl.num_programs(0), pl.num_programs(1)
    slot = i % 2

    def loads(tile, s):
        rows = pl.ds(pl.multiple_of(tile * FFN_TM, FFN_TM), FFN_TM)
        return (pltpu.make_async_copy(h1_hbm.at[rows, :], acc.at[s], load_sem.at[s, 0]),
                pltpu.make_async_copy(hn_hbm.at[rows, :], act.at[s], load_sem.at[s, 1]))

    def stores(tile, s):
        rows = pl.ds(pl.multiple_of(tile * FFN_TM, FFN_TM), FFN_TM)
        if last:
            return (pltpu.make_async_copy(acc.at[s], out_hbm.at[rows, :], store_sem.at[s, 0]),)
        return (pltpu.make_async_copy(acc.at[s], h2_hbm.at[rows, :], store_sem.at[s, 0]),
                pltpu.make_async_copy(act.at[s], xn_hbm.at[rows, :], store_sem.at[s, 1]))

    @pl.when((i == 0) & (f == 0))
    def _():
        for c in loads(0, 0):
            c.start()

    @pl.when(f == 0)
    def _():
        for c in loads(i, slot):
            c.wait()

    @pl.when((f == 1) & (i >= 1))
    def _():
        for c in stores(i - 1, 1 - slot):
            c.wait()

    @pl.when((f == 1) & (i + 1 < n_rows))
    def _():
        for c in loads(i + 1, 1 - slot):
            c.start()

    acc_s, act_s = acc.at[slot], act.at[slot]
    hn = act_s[...]
    acts = []
    for si in range(FFN_TF // FFN_SUB):
        sl = slice(si * FFN_SUB, (si + 1) * FFN_SUB)
        gate = jnp.dot(hn, wg_ref[:, sl], preferred_element_type=F32)
        up = jnp.dot(hn, wu_ref[:, sl], preferred_element_type=F32)
        acts.append((gate * (1.0 / (1.0 + jnp.exp(-gate))) * up).astype(BF16))
    a = jnp.concatenate(acts, axis=1)
    for ci in range(D_MODEL // FFN_DOWN_TN):
        cols = slice(ci * FFN_DOWN_TN, (ci + 1) * FFN_DOWN_TN)
        acc_s[:, cols] += jnp.dot(a, wd_ref[:, cols], preferred_element_type=F32)

    @pl.when(f == nf - 1)
    def _():
        y = _rmsnorm_rows(acc_s[...], gn_ref[...])
        if last:
            acc_s[...] = y
        else:
            act_s[...] = y.astype(BF16)
        for c in stores(i, slot):
            c.start()

    @pl.when((f == nf - 1) & (i == n_rows - 1))
    def _():
        for c in stores(i, slot):
            c.wait()


def _ffn(hn, h1, w_gu, w_down, g_next, last):
    T = h1.shape[0]
    nf = D_FF // FFN_TF
    hbm = pl.BlockSpec(memory_space=pl.ANY)
    rows_f32 = jax.ShapeDtypeStruct((T, D_MODEL), F32)
    return pl.pallas_call(
        functools.partial(_ffn_kernel, last), grid=(T // FFN_TM, nf),
        in_specs=[hbm, hbm,
                  pl.BlockSpec((None, D_MODEL, FFN_TF), lambda i, f: (0, 0, f)),
                  pl.BlockSpec((None, D_MODEL, FFN_TF), lambda i, f: (0, 0, nf + f)),
                  pl.BlockSpec((None, FFN_TF, D_MODEL), lambda i, f: (0, f, 0)),
                  pl.BlockSpec((1, D_MODEL), lambda i, f: (0, 0))],
        out_specs=hbm if last else [hbm, hbm],
        out_shape=rows_f32 if last else [rows_f32, jax.ShapeDtypeStruct((T, D_MODEL), BF16)],
        scratch_shapes=[pltpu.VMEM((2, FFN_TM, D_MODEL), F32), pltpu.VMEM((2, FFN_TM, D_MODEL), BF16),
                        pltpu.SemaphoreType.DMA((2, 2)), pltpu.SemaphoreType.DMA((2, 2))],
        compiler_params=_params(2), name="swiglu_ffn")(hn, h1, w_gu, w_gu, w_down, g_next)


def kernel(x, positions, norm1_g, w_in, b_in, sinks, sgu_ln_g, sgu_ln_b, sgu_w, sgu_b,
           w_attn_branch, w_sgu_branch, w_out, norm2_g, w_gate_up, w_down, final_g):
    B, S, D = x.shape
    T = B * S
    depth = w_in.shape[0]
    assert D == D_MODEL and S % ATT_TQ == 0 and w_in.shape[-1] == IN_WIDTH

    inv_freq = ROPE_THETA ** (-jnp.arange(0, ROPE_DIM, 2, dtype=F32) / ROPE_DIM)
    lane = jnp.arange(LANES) % HEAD_DIM
    invf_lanes = jnp.where(lane < ROPE_DIM, inv_freq[lane % ROPE_HALF], 0.0)[None, :]
    tabs = _rope_tables(positions.reshape(T, 1), invf_lanes)

    w_q, w_kv, w_z, w_g = (w_in[:1, :, lo:hi].astype(BF16) for lo, hi in IN_SEGMENTS)
    w_ab, w_sb, w_o = (w.astype(BF16) for w in (w_attn_branch, w_sgu_branch, w_out))
    sgu_w_b, sgu_b_c, g2 = sgu_w.astype(BF16), sgu_b[..., None], norm2_g[:, None, :]

    h = x.reshape(T, D)
    xn = _rmsnorm(h, norm1_g[0][None, :])
    for l in range(depth):
        b_q, b_kv, b_z, b_g = (b_in[l, lo:hi][None, None, :] for lo, hi in IN_SEGMENTS)
        gates, w_gu, w_dn = _proj(_gates_kernel, "proj_gates", xn, w_g, b_g, tn=1024,
                                  cast=((w_gate_up, w_down), l))
        z = _proj(_gelu_kernel, "proj_gelu", xn, w_z, b_z, tn=1024)
        q = _proj(_q_kernel, "proj_q", xn, w_q, b_q, tn=1024, tabs=tabs)
        kv = _proj(_kv_kernel, "proj_kv", xn, w_kv, b_kv, tn=512, out_scale=2, tabs=tabs)
        y_attn = _attention(q, kv, sinks[l], S)
        y_sgu = _spatial_gating(z, sgu_ln_g[l][None, :], sgu_ln_b[l][None, :], sgu_w_b[l], sgu_b_c[l])
        if l == depth - 1:
            h1, hn = _merge(y_attn, y_sgu, gates, h, w_ab, w_sb, w_o, g2, l)
            out = _ffn(hn, h1, w_gu, w_dn, final_g[None, :], last=True)
        else:
            h1, hn, w_q, w_kv, w_z, w_g = _merge(y_attn, y_sgu, gates, h, w_ab, w_sb, w_o, g2, l,
                                                 w_in32=w_in)
            h, xn = _ffn(hn, h1, w_gu, w_dn, norm1_g[l + 1][None, :], last=False)
    return out.reshape(B, S, D)
```

```python
import functools

import jax
import jax.numpy as jnp
from jax import lax
from jax.experimental import pallas as pl
from jax.experimental.pallas import tpu as pltpu

D_MODEL = 2048
N_Q_HEADS = 16
N_KV_HEADS = 4
HEAD_DIM = 64
Q_PER_KV = N_Q_HEADS // N_KV_HEADS
ATTN_WIDTH = N_Q_HEADS * HEAD_DIM
KV_WIDTH = N_KV_HEADS * HEAD_DIM
WINDOW = 128
ROPE_THETA = 500000.0
ROPE_DIM = HEAD_DIM // 4
ROPE_HALF = ROPE_DIM // 2
SGU_WIDTH = D_MODEL // 2
SGU_GROUPS = 8
SGU_GROUP_DIM = SGU_WIDTH // SGU_GROUPS
CHUNK = 128
D_FF = 5632
EPS = 1e-5
NEG = -1e30

OFF_Q = 0
OFF_K = OFF_Q + ATTN_WIDTH
OFF_Z = OFF_K + 2 * KV_WIDTH
OFF_G = OFF_Z + 2 * SGU_WIDTH
IN_WIDTH = OFF_G + 2 * D_MODEL
IN_SEGMENTS = ((OFF_Q, OFF_K), (OFF_K, OFF_Z), (OFF_Z, OFF_G), (OFF_G, IN_WIDTH))

LANES = 128
VMEM_LIMIT = 52 * 1024 * 1024

BF16 = jnp.bfloat16
F32 = jnp.float32


def _params(n_axes):
    return pltpu.CompilerParams(dimension_semantics=("arbitrary",) * n_axes,
                                vmem_limit_bytes=VMEM_LIMIT)


def _rmsnorm_rows(x, g):
    return x * lax.rsqrt(jnp.mean(x * x, axis=-1, keepdims=True) + EPS) * g


def _low_half_lanes(shape=(1, LANES)):
    return lax.broadcasted_iota(jnp.int32, shape, len(shape) - 1) % LANES < HEAD_DIM


def _rope_table_kernel(pos_ref, invf_ref, c_ref, s1_ref, s2_ref):
    lane = lax.broadcasted_iota(jnp.int32, c_ref.shape, 1) % HEAD_DIM
    ang = pos_ref[...].astype(F32) * invf_ref[...]
    cos, sin = jnp.cos(ang), jnp.sin(ang)
    first, second = lane < ROPE_HALF, (lane >= ROPE_HALF) & (lane < ROPE_DIM)
    c_ref[...] = jnp.where(lane < ROPE_DIM, cos, 1.0)
    s1_ref[...] = jnp.where(first, -sin, 0.0)
    s2_ref[...] = jnp.where(second, sin, 0.0)


def _rope_tables(pos, invf_lanes, tm=1024):
    T = pos.shape[0]
    tab = jax.ShapeDtypeStruct((T, LANES), F32)
    spec = pl.BlockSpec((tm, LANES), lambda i: (i, 0))
    return pl.pallas_call(
        _rope_table_kernel, grid=(T // tm,),
        in_specs=[pl.BlockSpec((tm, 1), lambda i: (i, 0)),
                  pl.BlockSpec((1, LANES), lambda i: (0, 0))],
        out_specs=[spec, spec, spec], out_shape=[tab, tab, tab],
        compiler_params=_params(1), name="rope_tables")(pos, invf_lanes)


def _cast_in_segments(w32_ref, seg_refs):
    for (lo, hi), seg_ref in zip(IN_SEGMENTS, seg_refs):
        seg_ref[...] = w32_ref[:, lo:hi].astype(BF16)


def _in_segment_specs(w_in32, layer, n_steps):
    rows = D_MODEL // n_steps
    in_spec = pl.BlockSpec((None, rows, IN_WIDTH), lambda i: (layer, i, 0))
    out_specs = [pl.BlockSpec((None, rows, hi - lo), lambda i: (0, i, 0)) for lo, hi in IN_SEGMENTS]
    out_shape = [jax.ShapeDtypeStruct((1, D_MODEL, hi - lo), BF16) for lo, hi in IN_SEGMENTS]
    return in_spec, out_specs, out_shape


def _norm_kernel(x_ref, g_ref, w32_ref, o_ref, *seg_refs):
    _cast_in_segments(w32_ref, seg_refs)
    o_ref[...] = _rmsnorm_rows(x_ref[...], g_ref[...]).astype(o_ref.dtype)


def _rmsnorm(x, g, w_in32, tm=512):
    T, D = x.shape
    w_spec, seg_specs, seg_shapes = _in_segment_specs(w_in32, 0, T // tm)
    return pl.pallas_call(
        _norm_kernel, grid=(T // tm,),
        in_specs=[pl.BlockSpec((tm, D), lambda i: (i, 0)),
                  pl.BlockSpec((1, D), lambda i: (0, 0)), w_spec],
        out_specs=[pl.BlockSpec((tm, D), lambda i: (i, 0))] + seg_specs,
        out_shape=[jax.ShapeDtypeStruct((T, D), BF16)] + seg_shapes,
        compiler_params=_params(1), name="rmsnorm0")(x, g, w_in32)


PROJ_TM = 1024
PROJ_SUB_M = 512
PROJ_SUB = 256


def _rope(r, c, s1, s2):
    return (r * c + pltpu.roll(r, LANES - ROPE_HALF, 1) * s1
            + pltpu.roll(r, ROPE_HALF, 1) * s2)


def _sub_tiles(xn_ref, w_ref, b_ref):
    for mi in range(xn_ref.shape[0] // PROJ_SUB_M):
        rows = slice(mi * PROJ_SUB_M, (mi + 1) * PROJ_SUB_M)
        xn = xn_ref[rows, :]
        for si in range(w_ref.shape[1] // PROJ_SUB):
            sl = slice(si * PROJ_SUB, (si + 1) * PROJ_SUB)
            yield rows, si, jnp.dot(xn, w_ref[:, sl], preferred_element_type=F32) + b_ref[:, sl]


def _cast_blocks(rest):
    n = len(rest) // 2
    for src, dst in zip(rest[:n], rest[n + 1:]):
        dst[...] = src[...].astype(BF16)
    return rest[n]


def _gates_kernel(xn_ref, w_ref, b_ref, *rest):
    o_ref = _cast_blocks(rest)
    for rows, si, r in _sub_tiles(xn_ref, w_ref, b_ref):
        o_ref[rows, si * PROJ_SUB:(si + 1) * PROJ_SUB] = (1.0 / (1.0 + jnp.exp(-r))).astype(o_ref.dtype)


def _gelu_kernel(xn_ref, w_ref, b_ref, *rest):
    o_ref = _cast_blocks(rest)
    for rows, si, r in _sub_tiles(xn_ref, w_ref, b_ref):
        o_ref[rows, si * PROJ_SUB:(si + 1) * PROJ_SUB] = (
            0.5 * r * (1.0 + lax.erf(r * (2.0 ** -0.5)))).astype(o_ref.dtype)


def _q_kernel(xn_ref, w_ref, b_ref, c_ref, s1_ref, s2_ref, o_ref):
    for rows, si, r in _sub_tiles(xn_ref, w_ref, b_ref):
        c, s1, s2 = (t[rows, :] * (HEAD_DIM ** -0.5) for t in (c_ref, s1_ref, s2_ref))
        for gi in range(PROJ_SUB // LANES):
            x = r[:, gi * LANES:(gi + 1) * LANES]
            lo = si * PROJ_SUB + gi * LANES
            o_ref[rows, lo:lo + LANES] = _rope(x, c, s1, s2).astype(o_ref.dtype)


def _kv_kernel(xn_ref, w_ref, b_ref, c_ref, s1_ref, s2_ref, o_ref):
    low = _low_half_lanes()
    for rows, si, r in _sub_tiles(xn_ref, w_ref, b_ref):
        for gi in range(PROJ_SUB // LANES):
            x = r[:, gi * LANES:(gi + 1) * LANES]
            if si * PROJ_SUB < KV_WIDTH:
                x = _rope(x, c_ref[rows, :], s1_ref[rows, :], s2_ref[rows, :])
            swapped = pltpu.roll(x, HEAD_DIM, 1)
            lo = 2 * (si * PROJ_SUB + gi * LANES)
            o_ref[rows, lo:lo + LANES] = jnp.where(low, x, swapped).astype(o_ref.dtype)
            o_ref[rows, lo + LANES:lo + 2 * LANES] = jnp.where(low, swapped, x).astype(o_ref.dtype)


def _proj(body, name, xn, w, b, tn, out_scale=1, tabs=(), cast=None):
    T, tm = xn.shape[0], PROJ_TM
    width = w.shape[-1]
    n_i, n_j = T // tm, width // tn
    tab_spec = pl.BlockSpec((tm, LANES), lambda i, j: (i, 0))
    in_specs = [pl.BlockSpec((tm, D_MODEL), lambda i, j: (i, 0)),
                pl.BlockSpec((None, D_MODEL, tn), lambda i, j: (0, 0, j)),
                pl.BlockSpec((None, 1, tn), lambda i, j: (0, 0, j))] + [tab_spec] * len(tabs)
    out_specs = [pl.BlockSpec((tm, out_scale * tn), lambda i, j: (i, j))]
    out_shape = [jax.ShapeDtypeStruct((T, out_scale * width), BF16)]
    args = [xn, w, b, *tabs]
    if cast is not None:
        arrays, layer = cast
        for a in arrays:
            blk = (None, a.shape[1] // n_i, a.shape[2] // n_j)
            in_specs.append(pl.BlockSpec(blk, lambda i, j: (layer, i, j)))
            out_specs.append(pl.BlockSpec(blk, lambda i, j: (0, i, j)))
            out_shape.append(jax.ShapeDtypeStruct((1,) + a.shape[1:], BF16))
            args.append(a)
    out = pl.pallas_call(
        body, grid=(n_i, n_j), in_specs=in_specs, out_specs=out_specs, out_shape=out_shape,
        compiler_params=_params(2), name=name)(*args)
    return out[0] if cast is None else out


ATT_TQ = 512
ATT_SUB = ATT_TQ // WINDOW
KVDUP_WIDTH = 2 * KV_WIDTH


def _attn_kernel(blocks_per_seq, sinks_ref, q_ref, kc_ref, vc_ref, kp_ref, vp_ref, o_ref):
    t = pl.program_id(0)
    qi = lax.broadcasted_iota(jnp.int32, (WINDOW, 2 * WINDOW), 0)
    kj = lax.broadcasted_iota(jnp.int32, (WINDOW, 2 * WINDOW), 1)
    rel = qi + WINDOW - kj
    band_ok = (rel >= 0) & (rel < WINDOW)
    low = _low_half_lanes()
    ones_low = jnp.broadcast_to(jnp.where(low, 1.0, 0.0).astype(BF16), (2 * WINDOW, LANES))
    ones_high = jnp.broadcast_to(jnp.where(low, 0.0, 1.0).astype(BF16), (2 * WINDOW, LANES))
    for c in range(ATT_SUB):
        rows = slice(c * WINDOW, (c + 1) * WINDOW)
        if c == 0:
            first_key = jnp.where((t * ATT_SUB) % blocks_per_seq == 0, WINDOW, 0)
            mask = band_ok & (kj >= first_key)
            k_prev, v_prev = kp_ref[...], vp_ref[...]
        else:
            mask = band_ok
            prev_rows = slice((c - 1) * WINDOW, c * WINDOW)
            k_prev, v_prev = kc_ref[prev_rows, :], vc_ref[prev_rows, :]
        k_band = jnp.concatenate([k_prev, kc_ref[rows, :]], axis=0)
        v_band = jnp.concatenate([v_prev, vc_ref[rows, :]], axis=0)
        for g in range(N_KV_HEADS):
            lanes_g = slice(g * LANES, (g + 1) * LANES)
            kg, vg = k_band[:, lanes_g], v_band[:, lanes_g]
            zero = jnp.zeros_like(vg)
            rhs_pv = jnp.concatenate(
                [jnp.concatenate([jnp.where(low, vg, zero), ones_low], axis=1),
                 jnp.concatenate([jnp.where(low, zero, vg), ones_high], axis=1)], axis=0)
            q_pairs = [q_ref[rows, (2 * g + i) * LANES:(2 * g + i + 1) * LANES] for i in range(2)]
            zq = jnp.zeros_like(q_pairs[0])
            lhs = jnp.concatenate([jnp.where(low, q_pairs[0], zq), jnp.where(low, zq, q_pairs[0]),
                                   jnp.where(low, q_pairs[1], zq), jnp.where(low, zq, q_pairs[1])],
                                  axis=0)
            s = lax.dot_general(lhs, kg, (((1,), (1,)), ((), ())), preferred_element_type=F32)
            probs, sink_terms = [], []
            for i in range(Q_PER_KV):
                sink = sinks_ref[g * Q_PER_KV + i]
                si = jnp.where(mask, s[i * WINDOW:(i + 1) * WINDOW, :], NEG)
                m = jnp.maximum(jnp.max(si, axis=-1, keepdims=True), sink)
                probs.append(jnp.exp(si - m).astype(BF16))
                sink_terms.append(jnp.exp(sink - m))
            for pair in range(2):
                p2 = jnp.concatenate(probs[2 * pair:2 * pair + 2], axis=1)
                r = jnp.dot(p2, rhs_pv, preferred_element_type=F32)
                den = r[:, LANES:] + jnp.where(low, sink_terms[2 * pair], sink_terms[2 * pair + 1])
                out_lanes = slice((2 * g + pair) * LANES, (2 * g + pair + 1) * LANES)
                o_ref[rows, out_lanes] = (r[:, :LANES] / den).astype(o_ref.dtype)


def _attention(q, kv, sinks, seq_len):
    T = q.shape[0]
    prev = lambda t: jnp.maximum(t * ATT_SUB - 1, 0)
    return pl.pallas_call(
        functools.partial(_attn_kernel, seq_len // WINDOW), grid=(T // ATT_TQ,),
        in_specs=[pl.BlockSpec(memory_space=pltpu.SMEM),
                  pl.BlockSpec((ATT_TQ, ATTN_WIDTH), lambda t: (t, 0)),
                  pl.BlockSpec((ATT_TQ, KVDUP_WIDTH), lambda t: (t, 0)),
                  pl.BlockSpec((ATT_TQ, KVDUP_WIDTH), lambda t: (t, 1)),
                  pl.BlockSpec((WINDOW, KVDUP_WIDTH), lambda t: (prev(t), 0)),
                  pl.BlockSpec((WINDOW, KVDUP_WIDTH), lambda t: (prev(t), 1))],
        out_specs=pl.BlockSpec((ATT_TQ, ATTN_WIDTH), lambda t: (t, 0)),
        out_shape=jax.ShapeDtypeStruct((T, ATTN_WIDTH), BF16),
        compiler_params=_params(1), name="swa_attention")(sinks, q, kv, kv, kv, kv)


SGU_TQ = 512


def _sgu_kernel(u_ref, v_ref, lng_ref, lnb_ref, w_ref, b_ref, o_ref):
    v = v_ref[...].astype(F32)
    mu = jnp.mean(v, axis=-1, keepdims=True)
    d = v - mu
    var = jnp.mean(d * d, axis=-1, keepdims=True)
    vn = (d * lax.rsqrt(var + EPS) * lng_ref[...] + lnb_ref[...]).astype(BF16)
    ti = lax.broadcasted_iota(jnp.int32, (CHUNK, CHUNK), 0)
    si = lax.broadcasted_iota(jnp.int32, (CHUNK, CHUNK), 1)
    causal = si <= ti
    for g in range(SGU_GROUPS):
        cols = slice(g * SGU_GROUP_DIM, (g + 1) * SGU_GROUP_DIM)
        w = jnp.where(causal, w_ref[g], jnp.zeros_like(w_ref[g]))
        bias = b_ref[g]
        for c in range(SGU_TQ // CHUNK):
            rows = slice(c * CHUNK, (c + 1) * CHUNK)
            sv = jnp.dot(w, vn[rows, cols], preferred_element_type=F32) + bias
            o_ref[rows, cols] = (u_ref[rows, cols].astype(F32) * sv).astype(o_ref.dtype)


def _spatial_gating(z, ln_g, ln_b, w, b):
    T = z.shape[0]
    full = lambda shape: pl.BlockSpec(shape, lambda t: (0,) * len(shape))
    return pl.pallas_call(
        _sgu_kernel, grid=(T // SGU_TQ,),
        in_specs=[pl.BlockSpec((SGU_TQ, SGU_WIDTH), lambda t: (t, 0)),
                  pl.BlockSpec((SGU_TQ, SGU_WIDTH), lambda t: (t, 1)),
                  full((1, SGU_WIDTH)), full((1, SGU_WIDTH)),
                  full((SGU_GROUPS, CHUNK, CHUNK)), full((SGU_GROUPS, CHUNK, 1))],
        out_specs=pl.BlockSpec((SGU_TQ, SGU_WIDTH), lambda t: (t, 0)),
        out_shape=jax.ShapeDtypeStruct((T, SGU_WIDTH), BF16),
        compiler_params=_params(1), name="spatial_gating")(z, z, ln_g, ln_b, w, b)


MRG_TN = 512


def _merge_kernel(ya_ref, ys_ref, g_ref, h_ref, wa_ref, ws_ref, wo_ref, g2_ref, *rest):
    if len(rest) == 3:
        h1_ref, hn_ref, merged_ref = rest
    else:
        next_in_ref, h1_ref, hn_ref, *seg_refs, merged_ref = rest
        _cast_in_segments(next_in_ref, seg_refs)
    ya, ys = ya_ref[...], ys_ref[...]
    for j in range(D_MODEL // MRG_TN):
        cols = slice(j * MRG_TN, (j + 1) * MRG_TN)
        gs_cols = slice(D_MODEL + j * MRG_TN, D_MODEL + (j + 1) * MRG_TN)
        a = jnp.dot(ya, wa_ref[:, cols], preferred_element_type=F32)
        s = jnp.dot(ys, ws_ref[:, cols], preferred_element_type=F32)
        merged_ref[:, cols] = (g_ref[:, cols].astype(F32) * a
                               + g_ref[:, gs_cols].astype(F32) * s).astype(BF16)
    merged = merged_ref[...]
    sum_sq = jnp.zeros((merged.shape[0], 1), F32)
    for j in range(D_MODEL // MRG_TN):
        cols = slice(j * MRG_TN, (j + 1) * MRG_TN)
        h1 = h_ref[:, cols] + jnp.dot(merged, wo_ref[:, cols], preferred_element_type=F32)
        h1_ref[:, cols] = h1
        sum_sq += jnp.sum(h1 * h1, axis=-1, keepdims=True)
    scale = lax.rsqrt(sum_sq * (1.0 / D_MODEL) + EPS)
    hn_ref[...] = (h1_ref[...] * scale * g2_ref[...]).astype(hn_ref.dtype)


def _merge(y_attn, y_sgu, gates, h, w_ab, w_sb, w_out, g2, cast_in=None, tm=256):
    T = h.shape[0]
    n_rows = T // tm
    row = lambda width: pl.BlockSpec((tm, width), lambda i: (i, 0))
    resident = lambda shape: pl.BlockSpec((None,) + shape, lambda i: (0, 0, 0),
                                          pipeline_mode=pl.Buffered(1))
    in_specs = [row(ATTN_WIDTH), row(SGU_WIDTH), row(2 * D_MODEL), row(D_MODEL),
                resident((ATTN_WIDTH, D_MODEL)), resident((SGU_WIDTH, D_MODEL)),
                resident((D_MODEL, D_MODEL)), resident((1, D_MODEL))]
    out_specs = [row(D_MODEL), row(D_MODEL)]
    out_shape = [jax.ShapeDtypeStruct((T, D_MODEL), F32), jax.ShapeDtypeStruct((T, D_MODEL), BF16)]
    args = [y_attn, y_sgu, gates, h, w_ab, w_sb, w_out, g2]
    if cast_in is not None:
        w_spec, seg_specs, seg_shapes = _in_segment_specs(*cast_in, n_rows)
        in_specs.append(w_spec)
        args.append(cast_in[0])
        out_specs += seg_specs
        out_shape += seg_shapes
    return pl.pallas_call(
        _merge_kernel, grid=(n_rows,), in_specs=in_specs, out_specs=out_specs, out_shape=out_shape,
        scratch_shapes=[pltpu.VMEM((tm, D_MODEL), BF16)],
        compiler_params=_params(1), name="merge_out_proj")(*args)


FFN_TM = 1024
FFN_TF = 512
FFN_SUB = 256
FFN_DOWN_TN = 512


def _ffn_kernel(last, hn_hbm, h1_hbm, wg_ref, wu_ref, wd_ref, gn_ref, *rest):
    if last:
        out_hbm, acc, act, load_sem, store_sem = rest
    else:
        h2_hbm, xn_hbm, acc, act, load_sem, store_sem = rest
    i, f = pl.program_id(0), pl.program_id(1)
    n_rows, nf = pl.num_programs(0), pl.num_programs(1)
    slot = i % 2

    def loads(tile, s):
        rows = pl.ds(pl.multiple_of(tile * FFN_TM, FFN_TM), FFN_TM)
        return (pltpu.make_async_copy(h1_hbm.at[rows, :], acc.at[s], load_sem.at[s, 0]),
                pltpu.make_async_copy(hn_hbm.at[rows, :], act.at[s], load_sem.at[s, 1]))

    def stores(tile, s):
        rows = pl.ds(pl.multiple_of(tile * FFN_TM, FFN_TM), FFN_TM)
        if last:
            return (pltpu.make_async_copy(acc.at[s], out_hbm.at[rows, :], store_sem.at[s, 0]),)
        return (pltpu.make_async_copy(acc.at[s], h2_hbm.at[rows, :], store_sem.at[s, 0]),
                pltpu.make_async_copy(act.at[s], xn_hbm.at[rows, :], store_sem.at[s, 1]))

    @pl.when((i == 0) & (f == 0))
    def _():
        for c in loads(0, 0):
            c.start()

    @pl.when(f == 0)
    def _():
        for c in loads(i, slot):
            c.wait()

    @pl.when((f == 1) & (i >= 1))
    def _():
        for c in stores(i - 1, 1 - slot):
            c.wait()

    @pl.when((f == 1) & (i + 1 < n_rows))
    def _():
        for c in loads(i + 1, 1 - slot):
            c.start()

    acc_s, act_s = acc.at[slot], act.at[slot]
    hn = act_s[...]
    acts = []
    for si in range(FFN_TF // FFN_SUB):
        sl = slice(si * FFN_SUB, (si + 1) * FFN_SUB)
        gate = jnp.dot(hn, wg_ref[:, sl], preferred_element_type=F32)
        up = jnp.dot(hn, wu_ref[:, sl], preferred_element_type=F32)
        acts.append((gate * (1.0 / (1.0 + jnp.exp(-gate))) * up).astype(BF16))
    a = jnp.concatenate(acts, axis=1)
    for ci in range(D_MODEL // FFN_DOWN_TN):
        cols = slice(ci * FFN_DOWN_TN, (ci + 1) * FFN_DOWN_TN)
        acc_s[:, cols] += jnp.dot(a, wd_ref[:, cols], preferred_element_type=F32)

    @pl.when(f == nf - 1)
    def _():
        y = _rmsnorm_rows(acc_s[...], gn_ref[...])
        if last:
            acc_s[...] = y
        else:
            act_s[...] = y.astype(BF16)
        for c in stores(i, slot):
            c.start()

    @pl.when((f == nf - 1) & (i == n_rows - 1))
    def _():
        for c in stores(i, slot):
            c.wait()


def _ffn(hn, h1, w_gu, w_down, g_next, last):
    T = h1.shape[0]
    nf = D_FF // FFN_TF
    hbm = pl.BlockSpec(memory_space=pl.ANY)
    rows_f32 = jax.ShapeDtypeStruct((T, D_MODEL), F32)
    return pl.pallas_call(
        functools.partial(_ffn_kernel, last), grid=(T // FFN_TM, nf),
        in_specs=[hbm, hbm,
                  pl.BlockSpec((None, D_MODEL, FFN_TF), lambda i, f: (0, 0, f)),
                  pl.BlockSpec((None, D_MODEL, FFN_TF), lambda i, f: (0, 0, nf + f)),
                  pl.BlockSpec((None, FFN_TF, D_MODEL), lambda i, f: (0, f, 0)),
                  pl.BlockSpec((1, D_MODEL), lambda i, f: (0, 0))],
        out_specs=hbm if last else [hbm, hbm],
        out_shape=rows_f32 if last else [rows_f32, jax.ShapeDtypeStruct((T, D_MODEL), BF16)],
        scratch_shapes=[pltpu.VMEM((2, FFN_TM, D_MODEL), F32), pltpu.VMEM((2, FFN_TM, D_MODEL), BF16),
                        pltpu.SemaphoreType.DMA((2, 2)), pltpu.SemaphoreType.DMA((2, 2))],
        compiler_params=_params(2), name="swiglu_ffn")(hn, h1, w_gu, w_gu, w_down, g_next)


def kernel(x, positions, norm1_g, w_in, b_in, sinks, sgu_ln_g, sgu_ln_b, sgu_w, sgu_b,
           w_attn_branch, w_sgu_branch, w_out, norm2_g, w_gate_up, w_down, final_g):
    B, S, D = x.shape
    T = B * S
    depth = w_in.shape[0]
    assert D == D_MODEL and S % ATT_TQ == 0 and w_in.shape[-1] == IN_WIDTH

    inv_freq = ROPE_THETA ** (-jnp.arange(0, ROPE_DIM, 2, dtype=F32) / ROPE_DIM)
    lane = jnp.arange(LANES) % HEAD_DIM
    invf_lanes = jnp.where(lane < ROPE_DIM, inv_freq[lane % ROPE_HALF], 0.0)[None, :]
    tabs = _rope_tables(positions.reshape(T, 1), invf_lanes)

    sgu_w_b, sgu_b_c = sgu_w.astype(BF16), sgu_b[..., None]

    h = x.reshape(T, D)
    xn, w_q, w_kv, w_z, w_g = _rmsnorm(h, norm1_g[0][None, :], w_in)
    for l in range(depth):
        b_q, b_kv, b_z, b_g = (b_in[l, lo:hi][None, None, :] for lo, hi in IN_SEGMENTS)
        gates, w_gu, w_dn = _proj(_gates_kernel, "proj_gates", xn, w_g, b_g, tn=1024,
                                  cast=((w_gate_up, w_down), l))
        z, w_ab, w_sb, w_o = _proj(_gelu_kernel, "proj_gelu", xn, w_z, b_z, tn=1024,
                                   cast=((w_attn_branch, w_sgu_branch, w_out), l))
        q = _proj(_q_kernel, "proj_q", xn, w_q, b_q, tn=1024, tabs=tabs)
        kv = _proj(_kv_kernel, "proj_kv", xn, w_kv, b_kv, tn=512, out_scale=2, tabs=tabs)
        y_attn = _attention(q, kv, sinks[l], S)
        y_sgu = _spatial_gating(z, sgu_ln_g[l][None, :], sgu_ln_b[l][None, :], sgu_w_b[l], sgu_b_c[l])
        g2 = norm2_g[l][None, None, :]
        if l == depth - 1:
            h1, hn = _merge(y_attn, y_sgu, gates, h, w_ab, w_sb, w_o, g2)
            out = _ffn(hn, h1, w_gu, w_dn, final_g[None, :], last=True)
        else:
            h1, hn, w_q, w_kv, w_z, w_g = _merge(y_attn, y_sgu, gates, h, w_ab, w_sb, w_o, g2,
                                                 cast_in=(w_in, l + 1))
            h, xn = _ffn(hn, h1, w_gu, w_dn, norm1_g[l + 1][None, :], last=False)
    return out.reshape(B, S, D)
```

```python
import functools

import jax
import jax.numpy as jnp
from jax import lax
from jax.experimental import pallas as pl
from jax.experimental.pallas import tpu as pltpu

D_MODEL = 2048
N_Q_HEADS = 16
N_KV_HEADS = 4
HEAD_DIM = 64
Q_PER_KV = N_Q_HEADS // N_KV_HEADS
ATTN_WIDTH = N_Q_HEADS * HEAD_DIM
KV_WIDTH = N_KV_HEADS * HEAD_DIM
WINDOW = 128
ROPE_THETA = 500000.0
ROPE_DIM = HEAD_DIM // 4
ROPE_HALF = ROPE_DIM // 2
SGU_WIDTH = D_MODEL // 2
SGU_GROUPS = 8
SGU_GROUP_DIM = SGU_WIDTH // SGU_GROUPS
CHUNK = 128
D_FF = 5632
EPS = 1e-5
NEG = -1e30

OFF_Q = 0
OFF_K = OFF_Q + ATTN_WIDTH
OFF_Z = OFF_K + 2 * KV_WIDTH
OFF_G = OFF_Z + 2 * SGU_WIDTH
IN_WIDTH = OFF_G + 2 * D_MODEL
IN_SEGMENTS = ((OFF_Q, OFF_Z), (OFF_Z, OFF_G), (OFF_G, IN_WIDTH))
KVDUP_WIDTH = 2 * KV_WIDTH

LANES = 128
VMEM_LIMIT = 52 * 1024 * 1024

BF16 = jnp.bfloat16
F32 = jnp.float32


def _params(n_axes):
    return pltpu.CompilerParams(dimension_semantics=("arbitrary",) * n_axes,
                                vmem_limit_bytes=VMEM_LIMIT)


def _rmsnorm_rows(x, g):
    return x * lax.rsqrt(jnp.mean(x * x, axis=-1, keepdims=True) + EPS) * g


def _low_half_lanes(shape=(1, LANES)):
    return lax.broadcasted_iota(jnp.int32, shape, len(shape) - 1) % LANES < HEAD_DIM


def _rope_table_kernel(pos_ref, invf_ref, c_ref, s1_ref, s2_ref):
    lane = lax.broadcasted_iota(jnp.int32, c_ref.shape, 1) % HEAD_DIM
    ang = pos_ref[...].astype(F32) * invf_ref[...]
    cos, sin = jnp.cos(ang), jnp.sin(ang)
    first, second = lane < ROPE_HALF, (lane >= ROPE_HALF) & (lane < ROPE_DIM)
    c_ref[...] = jnp.where(lane < ROPE_DIM, cos, 1.0)
    s1_ref[...] = jnp.where(first, -sin, 0.0)
    s2_ref[...] = jnp.where(second, sin, 0.0)


def _rope_tables(pos, invf_lanes, tm=1024):
    T = pos.shape[0]
    tab = jax.ShapeDtypeStruct((T, LANES), F32)
    spec = pl.BlockSpec((tm, LANES), lambda i: (i, 0))
    return pl.pallas_call(
        _rope_table_kernel, grid=(T // tm,),
        in_specs=[pl.BlockSpec((tm, 1), lambda i: (i, 0)),
                  pl.BlockSpec((1, LANES), lambda i: (0, 0))],
        out_specs=[spec, spec, spec], out_shape=[tab, tab, tab],
        compiler_params=_params(1), name="rope_tables")(pos, invf_lanes)


def _cast_in_segments(w32_ref, seg_refs):
    for (lo, hi), seg_ref in zip(IN_SEGMENTS, seg_refs):
        seg_ref[...] = w32_ref[:, lo:hi].astype(BF16)


def _in_segment_specs(w_in32, layer, n_steps):
    rows = D_MODEL // n_steps
    in_spec = pl.BlockSpec((None, rows, IN_WIDTH), lambda i: (layer, i, 0))
    out_specs = [pl.BlockSpec((None, rows, hi - lo), lambda i: (0, i, 0)) for lo, hi in IN_SEGMENTS]
    out_shape = [jax.ShapeDtypeStruct((1, D_MODEL, hi - lo), BF16) for lo, hi in IN_SEGMENTS]
    return in_spec, out_specs, out_shape


def _norm_kernel(x_ref, g_ref, w32_ref, o_ref, *seg_refs):
    _cast_in_segments(w32_ref, seg_refs)
    o_ref[...] = _rmsnorm_rows(x_ref[...], g_ref[...]).astype(o_ref.dtype)


def _rmsnorm(x, g, w_in32, tm=512):
    T, D = x.shape
    w_spec, seg_specs, seg_shapes = _in_segment_specs(w_in32, 0, T // tm)
    return pl.pallas_call(
        _norm_kernel, grid=(T // tm,),
        in_specs=[pl.BlockSpec((tm, D), lambda i: (i, 0)),
                  pl.BlockSpec((1, D), lambda i: (0, 0)), w_spec],
        out_specs=[pl.BlockSpec((tm, D), lambda i: (i, 0))] + seg_specs,
        out_shape=[jax.ShapeDtypeStruct((T, D), BF16)] + seg_shapes,
        compiler_params=_params(1), name="rmsnorm0")(x, g, w_in32)


PROJ_TM = 1024
PROJ_SUB_M = 512
PROJ_SUB = 256


def _rope(r, c, s1, s2):
    return (r * c + pltpu.roll(r, LANES - ROPE_HALF, 1) * s1
            + pltpu.roll(r, ROPE_HALF, 1) * s2)


def _sub_tiles(xn_ref, w_ref, b_ref):
    for mi in range(xn_ref.shape[0] // PROJ_SUB_M):
        rows = slice(mi * PROJ_SUB_M, (mi + 1) * PROJ_SUB_M)
        xn = xn_ref[rows, :]
        for si in range(w_ref.shape[1] // PROJ_SUB):
            sl = slice(si * PROJ_SUB, (si + 1) * PROJ_SUB)
            yield rows, si, jnp.dot(xn, w_ref[:, sl], preferred_element_type=F32) + b_ref[:, sl]


def _cast_blocks(rest):
    n = len(rest) // 2
    for src, dst in zip(rest[:n], rest[n + 1:]):
        dst[...] = src[...].astype(BF16)
    return rest[n]


def _gates_kernel(xn_ref, w_ref, b_ref, *rest):
    o_ref = _cast_blocks(rest)
    for rows, si, r in _sub_tiles(xn_ref, w_ref, b_ref):
        o_ref[rows, si * PROJ_SUB:(si + 1) * PROJ_SUB] = (1.0 / (1.0 + jnp.exp(-r))).astype(o_ref.dtype)


def _gelu_kernel(xn_ref, w_ref, b_ref, *rest):
    o_ref = _cast_blocks(rest)
    for rows, si, r in _sub_tiles(xn_ref, w_ref, b_ref):
        o_ref[rows, si * PROJ_SUB:(si + 1) * PROJ_SUB] = (
            0.5 * r * (1.0 + lax.erf(r * (2.0 ** -0.5)))).astype(o_ref.dtype)


def _qkv_kernel(xn_ref, w_ref, b_ref, c_ref, s1_ref, s2_ref, q_ref, kv_ref):
    low = _low_half_lanes()
    for rows, si, r in _sub_tiles(xn_ref, w_ref, b_ref):
        c, s1, s2 = c_ref[rows, :], s1_ref[rows, :], s2_ref[rows, :]
        for gi in range(PROJ_SUB // LANES):
            x = r[:, gi * LANES:(gi + 1) * LANES]
            col = si * PROJ_SUB + gi * LANES
            if col < ATTN_WIDTH:
                q_ref[rows, col:col + LANES] = (
                    _rope(x, c, s1, s2) * (HEAD_DIM ** -0.5)).astype(q_ref.dtype)
                continue
            if col < ATTN_WIDTH + KV_WIDTH:
                x = _rope(x, c, s1, s2)
            swapped = pltpu.roll(x, HEAD_DIM, 1)
            lo = 2 * (col - ATTN_WIDTH)
            kv_ref[rows, lo:lo + LANES] = jnp.where(low, x, swapped).astype(kv_ref.dtype)
            kv_ref[rows, lo + LANES:lo + 2 * LANES] = jnp.where(low, swapped, x).astype(kv_ref.dtype)


def _qkv_proj(xn, w, b, tabs):
    T, tm = xn.shape[0], PROJ_TM
    width = w.shape[-1]
    tab_spec = pl.BlockSpec((tm, LANES), lambda i: (i, 0))
    return pl.pallas_call(
        _qkv_kernel, grid=(T // tm,),
        in_specs=[pl.BlockSpec((tm, D_MODEL), lambda i: (i, 0)),
                  pl.BlockSpec((None, D_MODEL, width), lambda i: (0, 0, 0)),
                  pl.BlockSpec((None, 1, width), lambda i: (0, 0, 0))] + [tab_spec] * 3,
        out_specs=[pl.BlockSpec((tm, ATTN_WIDTH), lambda i: (i, 0)),
                   pl.BlockSpec((tm, 2 * KVDUP_WIDTH), lambda i: (i, 0))],
        out_shape=[jax.ShapeDtypeStruct((T, ATTN_WIDTH), BF16),
                   jax.ShapeDtypeStruct((T, 2 * KVDUP_WIDTH), BF16)],
        compiler_params=_params(1), name="proj_qkv")(xn, w, b, *tabs)


def _proj(body, name, xn, w, b, tn, cast):
    T, tm = xn.shape[0], PROJ_TM
    width = w.shape[-1]
    n_i, n_j = T // tm, width // tn
    in_specs = [pl.BlockSpec((tm, D_MODEL), lambda i, j: (i, 0)),
                pl.BlockSpec((None, D_MODEL, tn), lambda i, j: (0, 0, j)),
                pl.BlockSpec((None, 1, tn), lambda i, j: (0, 0, j))]
    out_specs = [pl.BlockSpec((tm, tn), lambda i, j: (i, j))]
    out_shape = [jax.ShapeDtypeStruct((T, width), BF16)]
    args = [xn, w, b]
    arrays, layer = cast
    for a in arrays:
        blk = (None, a.shape[1] // n_i, a.shape[2] // n_j)
        in_specs.append(pl.BlockSpec(blk, lambda i, j: (layer, i, j)))
        out_specs.append(pl.BlockSpec(blk, lambda i, j: (0, i, j)))
        out_shape.append(jax.ShapeDtypeStruct((1,) + a.shape[1:], BF16))
        args.append(a)
    return pl.pallas_call(
        body, grid=(n_i, n_j), in_specs=in_specs, out_specs=out_specs, out_shape=out_shape,
        compiler_params=_params(2), name=name)(*args)


def _attention_rows(seq_start, sinks_ref, q_ref, kc_ref, vc_ref, kp_ref, vp_ref, o_ref):
    qi = lax.broadcasted_iota(jnp.int32, (WINDOW, 2 * WINDOW), 0)
    kj = lax.broadcasted_iota(jnp.int32, (WINDOW, 2 * WINDOW), 1)
    rel = qi + WINDOW - kj
    band_ok = (rel >= 0) & (rel < WINDOW)
    low = _low_half_lanes()
    ones_low = jnp.broadcast_to(jnp.where(low, 1.0, 0.0).astype(BF16), (2 * WINDOW, LANES))
    ones_high = jnp.broadcast_to(jnp.where(low, 0.0, 1.0).astype(BF16), (2 * WINDOW, LANES))
    for c in range(q_ref.shape[0] // WINDOW):
        rows = slice(c * WINDOW, (c + 1) * WINDOW)
        if c == 0:
            first_key = jnp.where(seq_start, WINDOW, 0)
            mask = band_ok & (kj >= first_key)
            k_prev, v_prev = kp_ref[...], vp_ref[...]
        else:
            mask = band_ok
            prev_rows = slice((c - 1) * WINDOW, c * WINDOW)
            k_prev, v_prev = kc_ref[prev_rows, :], vc_ref[prev_rows, :]
        k_band = jnp.concatenate([k_prev, kc_ref[rows, :]], axis=0)
        v_band = jnp.concatenate([v_prev, vc_ref[rows, :]], axis=0)
        for g in range(N_KV_HEADS):
            lanes_g = slice(g * LANES, (g + 1) * LANES)
            kg, vg = k_band[:, lanes_g], v_band[:, lanes_g]
            zero = jnp.zeros_like(vg)
            rhs_pv = jnp.concatenate(
                [jnp.concatenate([jnp.where(low, vg, zero), ones_low], axis=1),
                 jnp.concatenate([jnp.where(low, zero, vg), ones_high], axis=1)], axis=0)
            q_pairs = [q_ref[rows, (2 * g + i) * LANES:(2 * g + i + 1) * LANES] for i in range(2)]
            zq = jnp.zeros_like(q_pairs[0])
            lhs = jnp.concatenate([jnp.where(low, q_pairs[0], zq), jnp.where(low, zq, q_pairs[0]),
                                   jnp.where(low, q_pairs[1], zq), jnp.where(low, zq, q_pairs[1])],
                                  axis=0)
            s = lax.dot_general(lhs, kg, (((1,), (1,)), ((), ())), preferred_element_type=F32)
            probs, sink_terms = [], []
            for i in range(Q_PER_KV):
                sink = sinks_ref[g * Q_PER_KV + i]
                si = jnp.where(mask, s[i * WINDOW:(i + 1) * WINDOW, :], NEG)
                m = jnp.maximum(jnp.max(si, axis=-1, keepdims=True), sink)
                probs.append(jnp.exp(si - m).astype(BF16))
                sink_terms.append(jnp.exp(sink - m))
            for pair in range(2):
                p2 = jnp.concatenate(probs[2 * pair:2 * pair + 2], axis=1)
                r = jnp.dot(p2, rhs_pv, preferred_element_type=F32)
                den = r[:, LANES:] + jnp.where(low, sink_terms[2 * pair], sink_terms[2 * pair + 1])
                out_lanes = slice((2 * g + pair) * LANES, (2 * g + pair + 1) * LANES)
                o_ref[rows, out_lanes] = (r[:, :LANES] / den).astype(o_ref.dtype)


def _sgu_rows(u_ref, v_ref, lng_ref, lnb_ref, w_ref, b_ref, o_ref):
    v = v_ref[...].astype(F32)
    mu = jnp.mean(v, axis=-1, keepdims=True)
    d = v - mu
    var = jnp.mean(d * d, axis=-1, keepdims=True)
    vn = (d * lax.rsqrt(var + EPS) * lng_ref[...] + lnb_ref[...]).astype(BF16)
    ti = lax.broadcasted_iota(jnp.int32, (CHUNK, CHUNK), 0)
    si = lax.broadcasted_iota(jnp.int32, (CHUNK, CHUNK), 1)
    causal = si <= ti
    for g in range(SGU_GROUPS):
        cols = slice(g * SGU_GROUP_DIM, (g + 1) * SGU_GROUP_DIM)
        w = jnp.where(causal, w_ref[g], jnp.zeros_like(w_ref[g]))
        bias = b_ref[g]
        for c in range(u_ref.shape[0] // CHUNK):
            rows = slice(c * CHUNK, (c + 1) * CHUNK)
            sv = jnp.dot(w, vn[rows, cols], preferred_element_type=F32) + bias
            o_ref[rows, cols] = (u_ref[rows, cols].astype(F32) * sv).astype(o_ref.dtype)


MIX_TM = 256
MRG_TN = 512


def _mixer_kernel(blocks_per_seq, sinks_ref, q_ref, kc_ref, vc_ref, kp_ref, vp_ref, u_ref, v_ref,
                  lng_ref, lnb_ref, sw_ref, sb_ref, g_ref, h_ref, wa_ref, ws_ref, wo_ref, g2_ref, *rest):
    if len(rest) == 6:
        h1_ref, hn_ref, ya_ref, ys_ref, part_ref, merged_ref = rest
    else:
        next_in_ref, h1_ref, hn_ref, *seg_refs, ya_ref, ys_ref, part_ref, merged_ref = rest
        _cast_in_segments(next_in_ref, seg_refs)
    seq_start = (pl.program_id(0) * (MIX_TM // WINDOW)) % blocks_per_seq == 0
    _sgu_rows(u_ref, v_ref, lng_ref, lnb_ref, sw_ref, sb_ref, ys_ref)
    ys = ys_ref[...]
    for j in range(D_MODEL // MRG_TN):
        cols = slice(j * MRG_TN, (j + 1) * MRG_TN)
        gs_cols = slice(D_MODEL + j * MRG_TN, D_MODEL + (j + 1) * MRG_TN)
        part_ref[:, cols] = g_ref[:, gs_cols].astype(F32) * jnp.dot(
            ys, ws_ref[:, cols], preferred_element_type=F32)
    _attention_rows(seq_start, sinks_ref, q_ref, kc_ref, vc_ref, kp_ref, vp_ref, ya_ref)
    ya = ya_ref[...]
    for j in range(D_MODEL // MRG_TN):
        cols = slice(j * MRG_TN, (j + 1) * MRG_TN)
        a = jnp.dot(ya, wa_ref[:, cols], preferred_element_type=F32)
        merged_ref[:, cols] = (g_ref[:, cols].astype(F32) * a + part_ref[:, cols]).astype(BF16)
    merged = merged_ref[...]
    sum_sq = jnp.zeros((merged.shape[0], 1), F32)
    for j in range(D_MODEL // MRG_TN):
        cols = slice(j * MRG_TN, (j + 1) * MRG_TN)
        h1 = h_ref[:, cols] + jnp.dot(merged, wo_ref[:, cols], preferred_element_type=F32)
        h1_ref[:, cols] = h1
        sum_sq += jnp.sum(h1 * h1, axis=-1, keepdims=True)
    scale = lax.rsqrt(sum_sq * (1.0 / D_MODEL) + EPS)
    hn_ref[...] = (h1_ref[...] * scale * g2_ref[...]).astype(hn_ref.dtype)


def _mixer(q, kv, z, gates, h, sinks, ln_g, ln_b, sgu_w, sgu_b, w_ab, w_sb, w_out, g2, seq_len,
           cast_in=None):
    T, tm = h.shape[0], MIX_TM
    n_rows = T // tm
    row = lambda width, col=0: pl.BlockSpec((tm, width), lambda i: (i, col))
    prev = lambda col: pl.BlockSpec((WINDOW, KVDUP_WIDTH),
                                    lambda i: (jnp.maximum(i * (tm // WINDOW) - 1, 0), col))
    const = lambda shape: pl.BlockSpec(shape, lambda i: (0,) * len(shape))
    resident = lambda shape: pl.BlockSpec((None,) + shape, lambda i: (0, 0, 0),
                                          pipeline_mode=pl.Buffered(1))
    in_specs = [pl.BlockSpec(memory_space=pltpu.SMEM),
                row(ATTN_WIDTH), row(KVDUP_WIDTH, 0), row(KVDUP_WIDTH, 1), prev(0), prev(1),
                row(SGU_WIDTH, 0), row(SGU_WIDTH, 1), const((1, SGU_WIDTH)), const((1, SGU_WIDTH)),
                const((SGU_GROUPS, CHUNK, CHUNK)), const((SGU_GROUPS, CHUNK, 1)),
                row(2 * D_MODEL), row(D_MODEL),
                resident((ATTN_WIDTH, D_MODEL)), resident((SGU_WIDTH, D_MODEL)),
                resident((D_MODEL, D_MODEL)), resident((1, D_MODEL))]
    out_specs = [row(D_MODEL), row(D_MODEL)]
    out_shape = [jax.ShapeDtypeStruct((T, D_MODEL), F32), jax.ShapeDtypeStruct((T, D_MODEL), BF16)]
    args = [sinks, q, kv, kv, kv, kv, z, z, ln_g, ln_b, sgu_w, sgu_b, gates, h, w_ab, w_sb, w_out, g2]
    if cast_in is not None:
        w_spec, seg_specs, seg_shapes = _in_segment_specs(*cast_in, n_rows)
        in_specs.append(w_spec)
        args.append(cast_in[0])
        out_specs += seg_specs
        out_shape += seg_shapes
    return pl.pallas_call(
        functools.partial(_mixer_kernel, seq_len // WINDOW), grid=(n_rows,),
        in_specs=in_specs, out_specs=out_specs, out_shape=out_shape,
        scratch_shapes=[pltpu.VMEM((tm, ATTN_WIDTH), BF16), pltpu.VMEM((tm, SGU_WIDTH), BF16),
                        pltpu.VMEM((tm, D_MODEL), F32), pltpu.VMEM((tm, D_MODEL), BF16)],
        compiler_params=_params(1), name="token_mixer")(*args)


FFN_TM = 1024
FFN_TF = 512
FFN_SUB = 256
FFN_DOWN_TN = 512


def _ffn_kernel(last, hn_hbm, h1_hbm, wg_ref, wu_ref, wd_ref, gn_ref, *rest):
    if last:
        out_hbm, acc, act, load_sem, store_sem = rest
    else:
        h2_hbm, xn_hbm, acc, act, load_sem, store_sem = rest
    i, f = pl.program_id(0), pl.program_id(1)
    n_rows, nf = pl.num_programs(0), pl.num_programs(1)
    slot = i % 2

    def loads(tile, s):
        rows = pl.ds(pl.multiple_of(tile * FFN_TM, FFN_TM), FFN_TM)
        return (pltpu.make_async_copy(h1_hbm.at[rows, :], acc.at[s], load_sem.at[s, 0]),
                pltpu.make_async_copy(hn_hbm.at[rows, :], act.at[s], load_sem.at[s, 1]))

    def stores(tile, s):
        rows = pl.ds(pl.multiple_of(tile * FFN_TM, FFN_TM), FFN_TM)
        if last:
            return (pltpu.make_async_copy(acc.at[s], out_hbm.at[rows, :], store_sem.at[s, 0]),)
        return (pltpu.make_async_copy(acc.at[s], h2_hbm.at[rows, :], store_sem.at[s, 0]),
                pltpu.make_async_copy(act.at[s], xn_hbm.at[rows, :], store_sem.at[s, 1]))

    @pl.when((i == 0) & (f == 0))
    def _():
        for c in loads(0, 0):
            c.start()

    @pl.when(f == 0)
    def _():
        for c in loads(i, slot):
            c.wait()

    @pl.when((f == 1) & (i >= 1))
    def _():
        for c in stores(i - 1, 1 - slot):
            c.wait()

    @pl.when((f == 1) & (i + 1 < n_rows))
    def _():
        for c in loads(i + 1, 1 - slot):
            c.start()

    acc_s, act_s = acc.at[slot], act.at[slot]
    hn = act_s[...]
    acts = []
    for si in range(FFN_TF // FFN_SUB):
        sl = slice(si * FFN_SUB, (si + 1) * FFN_SUB)
        gate = jnp.dot(hn, wg_ref[:, sl], preferred_element_type=F32)
        up = jnp.dot(hn, wu_ref[:, sl], preferred_element_type=F32)
        acts.append((gate * (1.0 / (1.0 + jnp.exp(-gate))) * up).astype(BF16))
    a = jnp.concatenate(acts, axis=1)
    for ci in range(D_MODEL // FFN_DOWN_TN):
        cols = slice(ci * FFN_DOWN_TN, (ci + 1) * FFN_DOWN_TN)
        acc_s[:, cols] += jnp.dot(a, wd_ref[:, cols], preferred_element_type=F32)

    @pl.when(f == nf - 1)
    def _():
        y = _rmsnorm_rows(acc_s[...], gn_ref[...])
        if last:
            acc_s[...] = y
        else:
            act_s[...] = y.astype(BF16)
        for c in stores(i, slot):
            c.start()

    @pl.when((f == nf - 1) & (i == n_rows - 1))
    def _():
        for c in stores(i, slot):
            c.wait()


def _ffn(hn, h1, w_gu, w_down, g_next, last):
    T = h1.shape[0]
    nf = D_FF // FFN_TF
    hbm = pl.BlockSpec(memory_space=pl.ANY)
    rows_f32 = jax.ShapeDtypeStruct((T, D_MODEL), F32)
    return pl.pallas_call(
        functools.partial(_ffn_kernel, last), grid=(T // FFN_TM, nf),
        in_specs=[hbm, hbm,
                  pl.BlockSpec((None, D_MODEL, FFN_TF), lambda i, f: (0, 0, f)),
                  pl.BlockSpec((None, D_MODEL, FFN_TF), lambda i, f: (0, 0, nf + f)),
                  pl.BlockSpec((None, FFN_TF, D_MODEL), lambda i, f: (0, f, 0)),
                  pl.BlockSpec((1, D_MODEL), lambda i, f: (0, 0))],
        out_specs=hbm if last else [hbm, hbm],
        out_shape=rows_f32 if last else [rows_f32, jax.ShapeDtypeStruct((T, D_MODEL), BF16)],
        scratch_shapes=[pltpu.VMEM((2, FFN_TM, D_MODEL), F32), pltpu.VMEM((2, FFN_TM, D_MODEL), BF16),
                        pltpu.SemaphoreType.DMA((2, 2)), pltpu.SemaphoreType.DMA((2, 2))],
        compiler_params=_params(2), name="swiglu_ffn")(hn, h1, w_gu, w_gu, w_down, g_next)


def kernel(x, positions, norm1_g, w_in, b_in, sinks, sgu_ln_g, sgu_ln_b, sgu_w, sgu_b,
           w_attn_branch, w_sgu_branch, w_out, norm2_g, w_gate_up, w_down, final_g):
    B, S, D = x.shape
    T = B * S
    depth = w_in.shape[0]
    assert D == D_MODEL and S % MIX_TM == 0 and T % FFN_TM == 0 and w_in.shape[-1] == IN_WIDTH

    inv_freq = ROPE_THETA ** (-jnp.arange(0, ROPE_DIM, 2, dtype=F32) / ROPE_DIM)
    lane = jnp.arange(LANES) % HEAD_DIM
    invf_lanes = jnp.where(lane < ROPE_DIM, inv_freq[lane % ROPE_HALF], 0.0)[None, :]
    tabs = _rope_tables(positions.reshape(T, 1), invf_lanes)

    sgu_w_b, sgu_b_c = sgu_w.astype(BF16), sgu_b[..., None]

    h = x.reshape(T, D)
    xn, w_qkv, w_z, w_g = _rmsnorm(h, norm1_g[0][None, :], w_in)
    for l in range(depth):
        b_qkv, b_z, b_g = (b_in[l, lo:hi][None, None, :] for lo, hi in IN_SEGMENTS)
        gates, w_gu, w_dn = _proj(_gates_kernel, "proj_gates", xn, w_g, b_g, tn=2048,
                                  cast=((w_gate_up, w_down), l))
        z, w_ab, w_sb, w_o = _proj(_gelu_kernel, "proj_gelu", xn, w_z, b_z, tn=2048,
                                   cast=((w_attn_branch, w_sgu_branch, w_out), l))
        q, kv = _qkv_proj(xn, w_qkv, b_qkv, tabs)
        mix = functools.partial(_mixer, q, kv, z, gates, h, sinks[l], sgu_ln_g[l][None, :],
                                sgu_ln_b[l][None, :], sgu_w_b[l], sgu_b_c[l], w_ab, w_sb, w_o,
                                norm2_g[l][None, None, :], S)
        if l == depth - 1:
            h1, hn = mix()
            out = _ffn(hn, h1, w_gu, w_dn, final_g[None, :], last=True)
        else:
            h1, hn, w_qkv, w_z, w_g = mix(cast_in=(w_in, l + 1))
            h, xn = _ffn(hn, h1, w_gu, w_dn, norm1_g[l + 1][None, :], last=False)
    return out.reshape(B, S, D)
```

```python
import functools

import jax
import jax.numpy as jnp
from jax import lax
from jax.experimental import pallas as pl
from jax.experimental.pallas import tpu as pltpu

D_MODEL = 2048
N_Q_HEADS = 16
N_KV_HEADS = 4
HEAD_DIM = 64
Q_PER_KV = N_Q_HEADS // N_KV_HEADS
ATTN_WIDTH = N_Q_HEADS * HEAD_DIM
KV_WIDTH = N_KV_HEADS * HEAD_DIM
WINDOW = 128
ROPE_THETA = 500000.0
ROPE_DIM = HEAD_DIM // 4
ROPE_HALF = ROPE_DIM // 2
SGU_WIDTH = D_MODEL // 2
SGU_GROUPS = 8
SGU_GROUP_DIM = SGU_WIDTH // SGU_GROUPS
CHUNK = 128
D_FF = 5632
EPS = 1e-5
NEG = -1e30
LOG2E = 1.4426950408889634

OFF_Q = 0
OFF_K = OFF_Q + ATTN_WIDTH
OFF_Z = OFF_K + 2 * KV_WIDTH
OFF_G = OFF_Z + 2 * SGU_WIDTH
IN_WIDTH = OFF_G + 2 * D_MODEL
IN_SEGMENTS = ((OFF_Q, OFF_Z), (OFF_Z, OFF_G), (OFF_G, IN_WIDTH))
KVDUP_WIDTH = 2 * KV_WIDTH

LANES = 128
VMEM_LIMIT = 52 * 1024 * 1024

BF16 = jnp.bfloat16
F32 = jnp.float32


def _params(n_axes):
    return pltpu.CompilerParams(dimension_semantics=("arbitrary",) * n_axes,
                                vmem_limit_bytes=VMEM_LIMIT)


def _rmsnorm_rows(x, g):
    return x * lax.rsqrt(jnp.mean(x * x, axis=-1, keepdims=True) + EPS) * g


def _low_half_lanes(shape=(1, LANES)):
    return lax.broadcasted_iota(jnp.int32, shape, len(shape) - 1) % LANES < HEAD_DIM


def _rope_tables(pos_ref, invf_ref, c_ref, s1_ref, s2_ref):
    lane = lax.broadcasted_iota(jnp.int32, c_ref.shape, 1) % HEAD_DIM
    ang = pos_ref[...].astype(F32) * invf_ref[...]
    cos, sin = jnp.cos(ang), jnp.sin(ang)
    first, second = lane < ROPE_HALF, (lane >= ROPE_HALF) & (lane < ROPE_DIM)
    c_ref[...] = jnp.where(lane < ROPE_DIM, cos, 1.0)
    s1_ref[...] = jnp.where(first, -sin, 0.0)
    s2_ref[...] = jnp.where(second, sin, 0.0)


def _cast_in_segments(w32_ref, seg_refs):
    for (lo, hi), seg_ref in zip(IN_SEGMENTS, seg_refs):
        seg_ref[...] = w32_ref[:, lo:hi].astype(BF16)


def _in_segment_specs(w_in32, layer, n_steps):
    rows = D_MODEL // n_steps
    in_spec = pl.BlockSpec((None, rows, IN_WIDTH), lambda i: (layer, i, 0))
    out_specs = [pl.BlockSpec((None, rows, hi - lo), lambda i: (0, i, 0)) for lo, hi in IN_SEGMENTS]
    out_shape = [jax.ShapeDtypeStruct((1, D_MODEL, hi - lo), BF16) for lo, hi in IN_SEGMENTS]
    return in_spec, out_specs, out_shape


def _norm_kernel(x_ref, g_ref, pos_ref, invf_ref, w32_ref, o_ref, c_ref, s1_ref, s2_ref, *seg_refs):
    _rope_tables(pos_ref, invf_ref, c_ref, s1_ref, s2_ref)
    _cast_in_segments(w32_ref, seg_refs)
    o_ref[...] = _rmsnorm_rows(x_ref[...], g_ref[...]).astype(o_ref.dtype)


def _rmsnorm(x, g, pos, invf_lanes, w_in32, tm=512):
    T, D = x.shape
    w_spec, seg_specs, seg_shapes = _in_segment_specs(w_in32, 0, T // tm)
    tab_spec = pl.BlockSpec((tm, LANES), lambda i: (i, 0))
    tab_shape = jax.ShapeDtypeStruct((T, LANES), F32)
    return pl.pallas_call(
        _norm_kernel, grid=(T // tm,),
        in_specs=[pl.BlockSpec((tm, D), lambda i: (i, 0)), pl.BlockSpec((1, D), lambda i: (0, 0)),
                  pl.BlockSpec((tm, 1), lambda i: (i, 0)), pl.BlockSpec((1, LANES), lambda i: (0, 0)),
                  w_spec],
        out_specs=[pl.BlockSpec((tm, D), lambda i: (i, 0))] + [tab_spec] * 3 + seg_specs,
        out_shape=[jax.ShapeDtypeStruct((T, D), BF16)] + [tab_shape] * 3 + seg_shapes,
        compiler_params=_params(1), name="rmsnorm0")(x, g, pos, invf_lanes, w_in32)


PROJ_TM = 1024
PROJ_SUB_M = 512
PROJ_SUB = 256


def _rope(r, c, s1, s2):
    return (r * c + pltpu.roll(r, LANES - ROPE_HALF, 1) * s1
            + pltpu.roll(r, ROPE_HALF, 1) * s2)


def _sub_tiles(xn_ref, w_ref, b_ref):
    for mi in range(xn_ref.shape[0] // PROJ_SUB_M):
        rows = slice(mi * PROJ_SUB_M, (mi + 1) * PROJ_SUB_M)
        xn = xn_ref[rows, :]
        for si in range(w_ref.shape[1] // PROJ_SUB):
            sl = slice(si * PROJ_SUB, (si + 1) * PROJ_SUB)
            yield rows, si, jnp.dot(xn, w_ref[:, sl], preferred_element_type=F32) + b_ref[:, sl]


def _cast_blocks(rest):
    n = len(rest) // 2
    for src, dst in zip(rest[:n], rest[n + 1:]):
        dst[...] = src[...].astype(BF16)
    return rest[n]


def _gates_kernel(xn_ref, w_ref, b_ref, *rest):
    o_ref = _cast_blocks(rest)
    for rows, si, r in _sub_tiles(xn_ref, w_ref, b_ref):
        o_ref[rows, si * PROJ_SUB:(si + 1) * PROJ_SUB] = (1.0 / (1.0 + jnp.exp(-r))).astype(o_ref.dtype)


def _gelu_kernel(xn_ref, w_ref, b_ref, *rest):
    o_ref = _cast_blocks(rest)
    for rows, si, r in _sub_tiles(xn_ref, w_ref, b_ref):
        o_ref[rows, si * PROJ_SUB:(si + 1) * PROJ_SUB] = (
            0.5 * r * (1.0 + lax.erf(r * (2.0 ** -0.5)))).astype(o_ref.dtype)


def _qkv_kernel(xn_ref, w_ref, b_ref, c_ref, s1_ref, s2_ref, q_ref, kv_ref):
    low = _low_half_lanes()
    for rows, si, r in _sub_tiles(xn_ref, w_ref, b_ref):
        c, s1, s2 = c_ref[rows, :], s1_ref[rows, :], s2_ref[rows, :]
        for gi in range(PROJ_SUB // LANES):
            x = r[:, gi * LANES:(gi + 1) * LANES]
            col = si * PROJ_SUB + gi * LANES
            if col < ATTN_WIDTH:
                xq = _rope(x, c, s1, s2) * (HEAD_DIM ** -0.5 * LOG2E)
                q_ref[rows, 2 * col:2 * col + LANES] = jnp.where(low, xq, 0.0).astype(q_ref.dtype)
                q_ref[rows, 2 * col + LANES:2 * col + 2 * LANES] = jnp.where(low, 0.0, xq).astype(q_ref.dtype)
                continue
            if col < ATTN_WIDTH + KV_WIDTH:
                x = _rope(x, c, s1, s2)
            swapped = pltpu.roll(x, HEAD_DIM, 1)
            lo = 2 * (col - ATTN_WIDTH)
            kv_ref[rows, lo:lo + LANES] = jnp.where(low, x, swapped).astype(kv_ref.dtype)
            kv_ref[rows, lo + LANES:lo + 2 * LANES] = jnp.where(low, swapped, x).astype(kv_ref.dtype)


def _qkv_proj(xn, w, b, tabs):
    T, tm = xn.shape[0], PROJ_TM
    width = w.shape[-1]
    tab_spec = pl.BlockSpec((tm, LANES), lambda i: (i, 0))
    return pl.pallas_call(
        _qkv_kernel, grid=(T // tm,),
        in_specs=[pl.BlockSpec((tm, D_MODEL), lambda i: (i, 0)),
                  pl.BlockSpec((None, D_MODEL, width), lambda i: (0, 0, 0)),
                  pl.BlockSpec((None, 1, width), lambda i: (0, 0, 0))] + [tab_spec] * 3,
        out_specs=[pl.BlockSpec((tm, 2 * ATTN_WIDTH), lambda i: (i, 0)),
                   pl.BlockSpec((tm, 2 * KVDUP_WIDTH), lambda i: (i, 0))],
        out_shape=[jax.ShapeDtypeStruct((T, 2 * ATTN_WIDTH), BF16),
                   jax.ShapeDtypeStruct((T, 2 * KVDUP_WIDTH), BF16)],
        compiler_params=_params(1), name="proj_qkv")(xn, w, b, *tabs)


def _proj(body, name, xn, w, b, tn, cast):
    T, tm = xn.shape[0], PROJ_TM
    width = w.shape[-1]
    n_i, n_j = T // tm, width // tn
    in_specs = [pl.BlockSpec((tm, D_MODEL), lambda i, j: (i, 0)),
                pl.BlockSpec((None, D_MODEL, tn), lambda i, j: (0, 0, j)),
                pl.BlockSpec((None, 1, tn), lambda i, j: (0, 0, j))]
    out_specs = [pl.BlockSpec((tm, tn), lambda i, j: (i, j))]
    out_shape = [jax.ShapeDtypeStruct((T, width), BF16)]
    args = [xn, w, b]
    arrays, layer = cast
    for a in arrays:
        blk = (None, a.shape[1] // n_i, a.shape[2] // n_j)
        in_specs.append(pl.BlockSpec(blk, lambda i, j: (layer, i, j)))
        out_specs.append(pl.BlockSpec(blk, lambda i, j: (0, i, j)))
        out_shape.append(jax.ShapeDtypeStruct((1,) + a.shape[1:], BF16))
        args.append(a)
    return pl.pallas_call(
        body, grid=(n_i, n_j), in_specs=in_specs, out_specs=out_specs, out_shape=out_shape,
        compiler_params=_params(2), name=name)(*args)


def _attention_rows(seq_start, sinks_ref, q_ref, kc_ref, vc_ref, kp_ref, vp_ref, o_ref):
    qi = lax.broadcasted_iota(jnp.int32, (WINDOW, 2 * WINDOW), 0)
    kj = lax.broadcasted_iota(jnp.int32, (WINDOW, 2 * WINDOW), 1)
    rel = qi + WINDOW - kj
    band_ok = (rel >= 0) & (rel < WINDOW)
    low = _low_half_lanes()
    ones_low = jnp.broadcast_to(jnp.where(low, 1.0, 0.0).astype(BF16), (2 * WINDOW, LANES))
    ones_high = jnp.broadcast_to(jnp.where(low, 0.0, 1.0).astype(BF16), (2 * WINDOW, LANES))
    for c in range(q_ref.shape[0] // WINDOW):
        rows = slice(c * WINDOW, (c + 1) * WINDOW)
        if c == 0:
            first_key = jnp.where(seq_start, WINDOW, 0)
            mask = band_ok & (kj >= first_key)
            k_prev, v_prev = kp_ref[...], vp_ref[...]
        else:
            mask = band_ok
            prev_rows = slice((c - 1) * WINDOW, c * WINDOW)
            k_prev, v_prev = kc_ref[prev_rows, :], vc_ref[prev_rows, :]
        k_band = jnp.concatenate([k_prev, kc_ref[rows, :]], axis=0)
        v_band = jnp.concatenate([v_prev, vc_ref[rows, :]], axis=0)
        for g in range(N_KV_HEADS):
            lanes_g = slice(g * LANES, (g + 1) * LANES)
            kg, vg = k_band[:, lanes_g], v_band[:, lanes_g]
            zero = jnp.zeros_like(vg)
            rhs_pv = jnp.concatenate(
                [jnp.concatenate([jnp.where(low, vg, zero), ones_low], axis=1),
                 jnp.concatenate([jnp.where(low, zero, vg), ones_high], axis=1)], axis=0)
            lhs = jnp.concatenate(
                [q_ref[rows, (Q_PER_KV * g + i) * LANES:(Q_PER_KV * g + i + 1) * LANES]
                 for i in range(Q_PER_KV)], axis=0)
            s = lax.dot_general(lhs, kg, (((1,), (1,)), ((), ())), preferred_element_type=F32)
            probs, sink_terms = [], []
            for i in range(Q_PER_KV):
                sink = sinks_ref[g * Q_PER_KV + i] * LOG2E
                si = jnp.where(mask, s[i * WINDOW:(i + 1) * WINDOW, :], NEG)
                m = jnp.maximum(jnp.max(si, axis=-1, keepdims=True), sink)
                probs.append(jnp.exp2(si - m).astype(BF16))
                sink_terms.append(jnp.exp2(sink - m))
            for pair in range(2):
                p2 = jnp.concatenate(probs[2 * pair:2 * pair + 2], axis=1)
                r = jnp.dot(p2, rhs_pv, preferred_element_type=F32)
                den = r[:, LANES:] + jnp.where(low, sink_terms[2 * pair], sink_terms[2 * pair + 1])
                out_lanes = slice((2 * g + pair) * LANES, (2 * g + pair + 1) * LANES)
                o_ref[rows, out_lanes] = (r[:, :LANES] / den).astype(o_ref.dtype)


def _sgu_norm(v_ref, lng_ref, lnb_ref):
    v = v_ref[...].astype(F32)
    mu = jnp.mean(v, axis=-1, keepdims=True)
    d = v - mu
    var = jnp.mean(d * d, axis=-1, keepdims=True)
    return (d * lax.rsqrt(var + EPS) * lng_ref[...] + lnb_ref[...]).astype(BF16)


def _sgu_group(g, vn, u_ref, w_ref, b_ref, o_ref):
    ti = lax.broadcasted_iota(jnp.int32, (CHUNK, CHUNK), 0)
    si = lax.broadcasted_iota(jnp.int32, (CHUNK, CHUNK), 1)
    cols = slice(g * SGU_GROUP_DIM, (g + 1) * SGU_GROUP_DIM)
    w = jnp.where(si <= ti, w_ref[g], jnp.zeros_like(w_ref[g]))
    bias = b_ref[g]
    for c in range(u_ref.shape[0] // CHUNK):
        rows = slice(c * CHUNK, (c + 1) * CHUNK)
        sv = jnp.dot(w, vn[rows, cols], preferred_element_type=F32) + bias
        o_ref[rows, cols] = (u_ref[rows, cols].astype(F32) * sv).astype(o_ref.dtype)


MIX_TM = 256
MRG_TN = 512


def _mixer_kernel(blocks_per_seq, sinks_ref, q_ref, kc_ref, vc_ref, kp_ref, vp_ref, u_ref, v_ref,
                  lng_ref, lnb_ref, sw_ref, sb_ref, g_ref, h_ref, wa_ref, ws_ref, wo_ref, g2_ref, *rest):
    if len(rest) == 6:
        h1_ref, hn_ref, ya_ref, ys_ref, part_ref, merged_ref = rest
    else:
        next_in_ref, h1_ref, hn_ref, *seg_refs, ya_ref, ys_ref, part_ref, merged_ref = rest
        _cast_in_segments(next_in_ref, seg_refs)
    seq_start = (pl.program_id(0) * (MIX_TM // WINDOW)) % blocks_per_seq == 0
    _attention_rows(seq_start, sinks_ref, q_ref, kc_ref, vc_ref, kp_ref, vp_ref, ya_ref)
    ya = ya_ref[...]
    vn = _sgu_norm(v_ref, lng_ref, lnb_ref)
    groups_per_tile = SGU_GROUPS // (D_MODEL // MRG_TN)
    for j in range(D_MODEL // MRG_TN):
        cols = slice(j * MRG_TN, (j + 1) * MRG_TN)
        part_ref[:, cols] = g_ref[:, cols].astype(F32) * jnp.dot(
            ya, wa_ref[:, cols], preferred_element_type=F32)
        for g in range(j * groups_per_tile, (j + 1) * groups_per_tile):
            _sgu_group(g, vn, u_ref, sw_ref, sb_ref, ys_ref)
    ys = ys_ref[...]
    for j in range(D_MODEL // MRG_TN):
        cols = slice(j * MRG_TN, (j + 1) * MRG_TN)
        gs_cols = slice(D_MODEL + j * MRG_TN, D_MODEL + (j + 1) * MRG_TN)
        s = jnp.dot(ys, ws_ref[:, cols], preferred_element_type=F32)
        merged_ref[:, cols] = (g_ref[:, gs_cols].astype(F32) * s + part_ref[:, cols]).astype(BF16)
    merged = merged_ref[...]
    sum_sq = jnp.zeros((merged.shape[0], 1), F32)
    for j in range(D_MODEL // MRG_TN):
        cols = slice(j * MRG_TN, (j + 1) * MRG_TN)
        h1 = h_ref[:, cols] + jnp.dot(merged, wo_ref[:, cols], preferred_element_type=F32)
        h1_ref[:, cols] = h1
        sum_sq += jnp.sum(h1 * h1, axis=-1, keepdims=True)
    scale = lax.rsqrt(sum_sq * (1.0 / D_MODEL) + EPS)
    hn_ref[...] = (h1_ref[...] * scale * g2_ref[...]).astype(hn_ref.dtype)


def _mixer(q, kv, z, gates, h, sinks, ln_g, ln_b, sgu_w, sgu_b, w_ab, w_sb, w_out, g2, seq_len,
           cast_in=None):
    T, tm = h.shape[0], MIX_TM
    n_rows = T // tm
    row = lambda width, col=0: pl.BlockSpec((tm, width), lambda i: (i, col))
    prev = lambda col: pl.BlockSpec((WINDOW, KVDUP_WIDTH),
                                    lambda i: (jnp.maximum(i * (tm // WINDOW) - 1, 0), col))
    const = lambda shape: pl.BlockSpec(shape, lambda i: (0,) * len(shape))
    resident = lambda shape: pl.BlockSpec((None,) + shape, lambda i: (0, 0, 0),
                                          pipeline_mode=pl.Buffered(1))
    in_specs = [pl.BlockSpec(memory_space=pltpu.SMEM),
                row(2 * ATTN_WIDTH), row(KVDUP_WIDTH, 0), row(KVDUP_WIDTH, 1), prev(0), prev(1),
                row(SGU_WIDTH, 0), row(SGU_WIDTH, 1), const((1, SGU_WIDTH)), const((1, SGU_WIDTH)),
                const((SGU_GROUPS, CHUNK, CHUNK)), const((SGU_GROUPS, CHUNK, 1)),
                row(2 * D_MODEL), row(D_MODEL),
                resident((ATTN_WIDTH, D_MODEL)), resident((SGU_WIDTH, D_MODEL)),
                resident((D_MODEL, D_MODEL)), resident((1, D_MODEL))]
    out_specs = [row(D_MODEL), row(D_MODEL)]
    out_shape = [jax.ShapeDtypeStruct((T, D_MODEL), F32), jax.ShapeDtypeStruct((T, D_MODEL), BF16)]
    args = [sinks, q, kv, kv, kv, kv, z, z, ln_g, ln_b, sgu_w, sgu_b, gates, h, w_ab, w_sb, w_out, g2]
    if cast_in is not None:
        w_spec, seg_specs, seg_shapes = _in_segment_specs(*cast_in, n_rows)
        in_specs.append(w_spec)
        args.append(cast_in[0])
        out_specs += seg_specs
        out_shape += seg_shapes
    return pl.pallas_call(
        functools.partial(_mixer_kernel, seq_len // WINDOW), grid=(n_rows,),
        in_specs=in_specs, out_specs=out_specs, out_shape=out_shape,
        scratch_shapes=[pltpu.VMEM((tm, ATTN_WIDTH), BF16), pltpu.VMEM((tm, SGU_WIDTH), BF16),
                        pltpu.VMEM((tm, D_MODEL), F32), pltpu.VMEM((tm, D_MODEL), BF16)],
        compiler_params=_params(1), name="token_mixer")(*args)


FFN_TM = 1024
FFN_TF = 512
FFN_SUB = 256
FFN_DOWN_TN = 512


def _ffn_kernel(last, hn_hbm, h1_hbm, wg_ref, wu_ref, wd_ref, gn_ref, *rest):
    if last:
        out_hbm, acc, act, load_sem, store_sem = rest
    else:
        h2_hbm, xn_hbm, acc, act, load_sem, store_sem = rest
    i, f = pl.program_id(0), pl.program_id(1)
    n_rows, nf = pl.num_programs(0), pl.num_programs(1)
    slot = i % 2

    def loads(tile, s):
        rows = pl.ds(pl.multiple_of(tile * FFN_TM, FFN_TM), FFN_TM)
        return (pltpu.make_async_copy(h1_hbm.at[rows, :], acc.at[s], load_sem.at[s, 0]),
                pltpu.make_async_copy(hn_hbm.at[rows, :], act.at[s], load_sem.at[s, 1]))

    def stores(tile, s):
        rows = pl.ds(pl.multiple_of(tile * FFN_TM, FFN_TM), FFN_TM)
        if last:
            return (pltpu.make_async_copy(acc.at[s], out_hbm.at[rows, :], store_sem.at[s, 0]),)
        return (pltpu.make_async_copy(acc.at[s], h2_hbm.at[rows, :], store_sem.at[s, 0]),
                pltpu.make_async_copy(act.at[s], xn_hbm.at[rows, :], store_sem.at[s, 1]))

    @pl.when((i == 0) & (f == 0))
    def _():
        for c in loads(0, 0):
            c.start()

    @pl.when(f == 0)
    def _():
        for c in loads(i, slot):
            c.wait()

    @pl.when((f == 1) & (i >= 1))
    def _():
        for c in stores(i - 1, 1 - slot):
            c.wait()

    @pl.when((f == 1) & (i + 1 < n_rows))
    def _():
        for c in loads(i + 1, 1 - slot):
            c.start()

    acc_s, act_s = acc.at[slot], act.at[slot]
    hn = act_s[...]
    acts = []
    for si in range(FFN_TF // FFN_SUB):
        sl = slice(si * FFN_SUB, (si + 1) * FFN_SUB)
        gate = jnp.dot(hn, wg_ref[:, sl], preferred_element_type=F32)
        up = jnp.dot(hn, wu_ref[:, sl], preferred_element_type=F32)
        acts.append((gate * (1.0 / (1.0 + jnp.exp(-gate))) * up).astype(BF16))
    a = jnp.concatenate(acts, axis=1)
    for ci in range(D_MODEL // FFN_DOWN_TN):
        cols = slice(ci * FFN_DOWN_TN, (ci + 1) * FFN_DOWN_TN)
        acc_s[:, cols] += jnp.dot(a, wd_ref[:, cols], preferred_element_type=F32)

    @pl.when(f == nf - 1)
    def _():
        y = _rmsnorm_rows(acc_s[...], gn_ref[...])
        if last:
            acc_s[...] = y
        else:
            act_s[...] = y.astype(BF16)
        for c in stores(i, slot):
            c.start()

    @pl.when((f == nf - 1) & (i == n_rows - 1))
    def _():
        for c in stores(i, slot):
            c.wait()


def _ffn(hn, h1, w_gu, w_down, g_next, last):
    T = h1.shape[0]
    nf = D_FF // FFN_TF
    hbm = pl.BlockSpec(memory_space=pl.ANY)
    rows_f32 = jax.ShapeDtypeStruct((T, D_MODEL), F32)
    return pl.pallas_call(
        functools.partial(_ffn_kernel, last), grid=(T // FFN_TM, nf),
        in_specs=[hbm, hbm,
                  pl.BlockSpec((None, D_MODEL, FFN_TF), lambda i, f: (0, 0, f)),
                  pl.BlockSpec((None, D_MODEL, FFN_TF), lambda i, f: (0, 0, nf + f)),
                  pl.BlockSpec((None, FFN_TF, D_MODEL), lambda i, f: (0, f, 0)),
                  pl.BlockSpec((1, D_MODEL), lambda i, f: (0, 0))],
        out_specs=hbm if last else [hbm, hbm],
        out_shape=rows_f32 if last else [rows_f32, jax.ShapeDtypeStruct((T, D_MODEL), BF16)],
        scratch_shapes=[pltpu.VMEM((2, FFN_TM, D_MODEL), F32), pltpu.VMEM((2, FFN_TM, D_MODEL), BF16),
                        pltpu.SemaphoreType.DMA((2, 2)), pltpu.SemaphoreType.DMA((2, 2))],
        compiler_params=_params(2), name="swiglu_ffn")(hn, h1, w_gu, w_gu, w_down, g_next)


def kernel(x, positions, norm1_g, w_in, b_in, sinks, sgu_ln_g, sgu_ln_b, sgu_w, sgu_b,
           w_attn_branch, w_sgu_branch, w_out, norm2_g, w_gate_up, w_down, final_g):
    B, S, D = x.shape
    T = B * S
    depth = w_in.shape[0]
    assert D == D_MODEL and S % MIX_TM == 0 and T % FFN_TM == 0 and w_in.shape[-1] == IN_WIDTH

    inv_freq = ROPE_THETA ** (-jnp.arange(0, ROPE_DIM, 2, dtype=F32) / ROPE_DIM)
    lane = jnp.arange(LANES) % HEAD_DIM
    invf_lanes = jnp.where(lane < ROPE_DIM, inv_freq[lane % ROPE_HALF], 0.0)[None, :]

    sgu_w_b, sgu_b_c = sgu_w.astype(BF16), sgu_b[..., None]

    h = x.reshape(T, D)
    xn, *tabs, w_qkv, w_z, w_g = _rmsnorm(h, norm1_g[0][None, :], positions.reshape(T, 1), invf_lanes, w_in)
    for l in range(depth):
        b_qkv, b_z, b_g = (b_in[l, lo:hi][None, None, :] for lo, hi in IN_SEGMENTS)
        gates, w_gu, w_dn = _proj(_gates_kernel, "proj_gates", xn, w_g, b_g, tn=2048,
                                  cast=((w_gate_up, w_down), l))
        z, w_ab, w_sb, w_o = _proj(_gelu_kernel, "proj_gelu", xn, w_z, b_z, tn=2048,
                                   cast=((w_attn_branch, w_sgu_branch, w_out), l))
        q, kv = _qkv_proj(xn, w_qkv, b_qkv, tabs)
        mix = functools.partial(_mixer, q, kv, z, gates, h, sinks[l], sgu_ln_g[l][None, :],
                                sgu_ln_b[l][None, :], sgu_w_b[l], sgu_b_c[l], w_ab, w_sb, w_o,
                                norm2_g[l][None, None, :], S)
        if l == depth - 1:
            h1, hn = mix()
            out = _ffn(hn, h1, w_gu, w_dn, final_g[None, :], last=True)
        else:
            h1, hn, w_qkv, w_z, w_g = mix(cast_in=(w_in, l + 1))
            h, xn = _ffn(hn, h1, w_gu, w_dn, norm1_g[l + 1][None, :], last=False)
    return out.reshape(B, S, D)
```

```python
import functools

import jax
import jax.numpy as jnp
from jax import lax
from jax.experimental import pallas as pl
from jax.experimental.pallas import tpu as pltpu

D_MODEL = 2048
N_Q_HEADS = 16
N_KV_HEADS = 4
HEAD_DIM = 64
Q_PER_KV = N_Q_HEADS // N_KV_HEADS
ATTN_WIDTH = N_Q_HEADS * HEAD_DIM
KV_WIDTH = N_KV_HEADS * HEAD_DIM
WINDOW = 128
ROPE_THETA = 500000.0
ROPE_DIM = HEAD_DIM // 4
ROPE_HALF = ROPE_DIM // 2
SGU_WIDTH = D_MODEL // 2
SGU_GROUPS = 8
SGU_GROUP_DIM = SGU_WIDTH // SGU_GROUPS
CHUNK = 128
D_FF = 5632
EPS = 1e-5
NEG = -1e30
LOG2E = 1.4426950408889634

OFF_Q = 0
OFF_K = OFF_Q + ATTN_WIDTH
OFF_Z = OFF_K + 2 * KV_WIDTH
OFF_G = OFF_Z + 2 * SGU_WIDTH
IN_WIDTH = OFF_G + 2 * D_MODEL
IN_SEGMENTS = ((OFF_Q, OFF_Z), (OFF_Z, OFF_G), (OFF_G, IN_WIDTH))
KVDUP_WIDTH = 2 * KV_WIDTH

LANES = 128
VMEM_LIMIT = 52 * 1024 * 1024

BF16 = jnp.bfloat16
F32 = jnp.float32


def _params(n_axes):
    return pltpu.CompilerParams(dimension_semantics=("arbitrary",) * n_axes,
                                vmem_limit_bytes=VMEM_LIMIT)


def _rmsnorm_rows(x, g):
    return x * lax.rsqrt(jnp.mean(x * x, axis=-1, keepdims=True) + EPS) * g


def _low_half_lanes(shape=(1, LANES)):
    return lax.broadcasted_iota(jnp.int32, shape, len(shape) - 1) % LANES < HEAD_DIM


def _rope_tables(pos_ref, invf_ref, c_ref, s1_ref, s2_ref):
    lane = lax.broadcasted_iota(jnp.int32, c_ref.shape, 1) % HEAD_DIM
    ang = pos_ref[...].astype(F32) * invf_ref[...]
    cos, sin = jnp.cos(ang), jnp.sin(ang)
    first, second = lane < ROPE_HALF, (lane >= ROPE_HALF) & (lane < ROPE_DIM)
    c_ref[...] = jnp.where(lane < ROPE_DIM, cos, 1.0)
    s1_ref[...] = jnp.where(first, -sin, 0.0)
    s2_ref[...] = jnp.where(second, sin, 0.0)


def _cast_in_segments(w32_ref, seg_refs):
    for (lo, hi), seg_ref in zip(IN_SEGMENTS, seg_refs):
        seg_ref[...] = w32_ref[:, lo:hi].astype(BF16)


def _in_segment_specs(w_in32, layer, n_steps):
    rows = D_MODEL // n_steps
    in_spec = pl.BlockSpec((None, rows, IN_WIDTH), lambda i: (layer, i, 0))
    out_specs = [pl.BlockSpec((None, rows, hi - lo), lambda i: (0, i, 0)) for lo, hi in IN_SEGMENTS]
    out_shape = [jax.ShapeDtypeStruct((1, D_MODEL, hi - lo), BF16) for lo, hi in IN_SEGMENTS]
    return in_spec, out_specs, out_shape


def _norm_kernel(x_ref, g_ref, pos_ref, invf_ref, w32_ref, o_ref, c_ref, s1_ref, s2_ref, *seg_refs):
    _rope_tables(pos_ref, invf_ref, c_ref, s1_ref, s2_ref)
    _cast_in_segments(w32_ref, seg_refs)
    o_ref[...] = _rmsnorm_rows(x_ref[...], g_ref[...]).astype(o_ref.dtype)


def _rmsnorm(x, g, pos, invf_lanes, w_in32, tm=512):
    T, D = x.shape
    w_spec, seg_specs, seg_shapes = _in_segment_specs(w_in32, 0, T // tm)
    tab_spec = pl.BlockSpec((tm, LANES), lambda i: (i, 0))
    tab_shape = jax.ShapeDtypeStruct((T, LANES), F32)
    return pl.pallas_call(
        _norm_kernel, grid=(T // tm,),
        in_specs=[pl.BlockSpec((tm, D), lambda i: (i, 0)), pl.BlockSpec((1, D), lambda i: (0, 0)),
                  pl.BlockSpec((tm, 1), lambda i: (i, 0)), pl.BlockSpec((1, LANES), lambda i: (0, 0)),
                  w_spec],
        out_specs=[pl.BlockSpec((tm, D), lambda i: (i, 0))] + [tab_spec] * 3 + seg_specs,
        out_shape=[jax.ShapeDtypeStruct((T, D), BF16)] + [tab_shape] * 3 + seg_shapes,
        compiler_params=_params(1), name="rmsnorm0")(x, g, pos, invf_lanes, w_in32)


PROJ_TM = 1024
PROJ_SUB_M = 128
PROJ_SUB = 256


def _rope(r, c, s1, s2):
    return (r * c + pltpu.roll(r, LANES - ROPE_HALF, 1) * s1
            + pltpu.roll(r, ROPE_HALF, 1) * s2)


def _sub_tiles(xn_ref, w_ref, b_ref):
    for mi in range(xn_ref.shape[0] // PROJ_SUB_M):
        rows = slice(mi * PROJ_SUB_M, (mi + 1) * PROJ_SUB_M)
        xn = xn_ref[rows, :]
        for si in range(w_ref.shape[1] // PROJ_SUB):
            sl = slice(si * PROJ_SUB, (si + 1) * PROJ_SUB)
            yield rows, si, jnp.dot(xn, w_ref[:, sl], preferred_element_type=F32) + b_ref[:, sl]


def _cast_blocks(rest):
    n = len(rest) // 2
    for src, dst in zip(rest[:n], rest[n + 1:]):
        dst[...] = src[...].astype(BF16)
    return rest[n]


def _gates_kernel(xn_ref, w_ref, b_ref, *rest):
    o_ref = _cast_blocks(rest)
    for rows, si, r in _sub_tiles(xn_ref, w_ref, b_ref):
        o_ref[rows, si * PROJ_SUB:(si + 1) * PROJ_SUB] = (1.0 / (1.0 + jnp.exp(-r))).astype(o_ref.dtype)


def _gelu_kernel(xn_ref, w_ref, b_ref, *rest):
    o_ref = _cast_blocks(rest)
    for rows, si, r in _sub_tiles(xn_ref, w_ref, b_ref):
        o_ref[rows, si * PROJ_SUB:(si + 1) * PROJ_SUB] = (
            0.5 * r * (1.0 + lax.erf(r * (2.0 ** -0.5)))).astype(o_ref.dtype)


def _qkv_kernel(xn_ref, w_ref, b_ref, c_ref, s1_ref, s2_ref, q_ref, kv_ref):
    low = _low_half_lanes()
    for rows, si, r in _sub_tiles(xn_ref, w_ref, b_ref):
        c, s1, s2 = c_ref[rows, :], s1_ref[rows, :], s2_ref[rows, :]
        for gi in range(PROJ_SUB // LANES):
            x = r[:, gi * LANES:(gi + 1) * LANES]
            col = si * PROJ_SUB + gi * LANES
            if col < ATTN_WIDTH:
                xq = _rope(x, c, s1, s2) * (HEAD_DIM ** -0.5 * LOG2E)
                q_ref[rows, 2 * col:2 * col + LANES] = jnp.where(low, xq, 0.0).astype(q_ref.dtype)
                q_ref[rows, 2 * col + LANES:2 * col + 2 * LANES] = jnp.where(low, 0.0, xq).astype(q_ref.dtype)
                continue
            if col < ATTN_WIDTH + KV_WIDTH:
                x = _rope(x, c, s1, s2)
            swapped = pltpu.roll(x, HEAD_DIM, 1)
            lo = 2 * (col - ATTN_WIDTH)
            kv_ref[rows, lo:lo + LANES] = jnp.where(low, x, swapped).astype(kv_ref.dtype)
            kv_ref[rows, lo + LANES:lo + 2 * LANES] = jnp.where(low, swapped, x).astype(kv_ref.dtype)


def _qkv_proj(xn, w, b, tabs):
    T, tm = xn.shape[0], PROJ_TM
    width = w.shape[-1]
    tab_spec = pl.BlockSpec((tm, LANES), lambda i: (i, 0))
    return pl.pallas_call(
        _qkv_kernel, grid=(T // tm,),
        in_specs=[pl.BlockSpec((tm, D_MODEL), lambda i: (i, 0)),
                  pl.BlockSpec((None, D_MODEL, width), lambda i: (0, 0, 0)),
                  pl.BlockSpec((None, 1, width), lambda i: (0, 0, 0))] + [tab_spec] * 3,
        out_specs=[pl.BlockSpec((tm, 2 * ATTN_WIDTH), lambda i: (i, 0)),
                   pl.BlockSpec((tm, 2 * KVDUP_WIDTH), lambda i: (i, 0))],
        out_shape=[jax.ShapeDtypeStruct((T, 2 * ATTN_WIDTH), BF16),
                   jax.ShapeDtypeStruct((T, 2 * KVDUP_WIDTH), BF16)],
        compiler_params=_params(1), name="proj_qkv")(xn, w, b, *tabs)


def _proj(body, name, xn, w, b, tn, cast):
    T, tm = xn.shape[0], PROJ_TM
    width = w.shape[-1]
    n_i, n_j = T // tm, width // tn
    in_specs = [pl.BlockSpec((tm, D_MODEL), lambda i, j: (i, 0)),
                pl.BlockSpec((None, D_MODEL, tn), lambda i, j: (0, 0, j)),
                pl.BlockSpec((None, 1, tn), lambda i, j: (0, 0, j))]
    out_specs = [pl.BlockSpec((tm, tn), lambda i, j: (i, j))]
    out_shape = [jax.ShapeDtypeStruct((T, width), BF16)]
    args = [xn, w, b]
    arrays, layer = cast
    for a in arrays:
        blk = (None, a.shape[1] // n_i, a.shape[2] // n_j)
        in_specs.append(pl.BlockSpec(blk, lambda i, j: (layer, i, j)))
        out_specs.append(pl.BlockSpec(blk, lambda i, j: (0, i, j)))
        out_shape.append(jax.ShapeDtypeStruct((1,) + a.shape[1:], BF16))
        args.append(a)
    return pl.pallas_call(
        body, grid=(n_i, n_j), in_specs=in_specs, out_specs=out_specs, out_shape=out_shape,
        compiler_params=_params(2), name=name)(*args)


def _attention_rows(seq_start, sinks_ref, q_ref, kc_ref, vc_ref, kp_ref, vp_ref, o_ref):
    qi = lax.broadcasted_iota(jnp.int32, (WINDOW, 2 * WINDOW), 0)
    kj = lax.broadcasted_iota(jnp.int32, (WINDOW, 2 * WINDOW), 1)
    rel = qi + WINDOW - kj
    band_ok = (rel >= 0) & (rel < WINDOW)
    low = _low_half_lanes()
    ones_low = jnp.broadcast_to(jnp.where(low, 1.0, 0.0).astype(BF16), (2 * WINDOW, LANES))
    ones_high = jnp.broadcast_to(jnp.where(low, 0.0, 1.0).astype(BF16), (2 * WINDOW, LANES))
    for c in range(q_ref.shape[0] // WINDOW):
        rows = slice(c * WINDOW, (c + 1) * WINDOW)
        if c == 0:
            first_key = jnp.where(seq_start, WINDOW, 0)
            mask = band_ok & (kj >= first_key)
            k_prev, v_prev = kp_ref[...], vp_ref[...]
        else:
            mask = band_ok
            prev_rows = slice((c - 1) * WINDOW, c * WINDOW)
            k_prev, v_prev = kc_ref[prev_rows, :], vc_ref[prev_rows, :]
        k_band = jnp.concatenate([k_prev, kc_ref[rows, :]], axis=0)
        v_band = jnp.concatenate([v_prev, vc_ref[rows, :]], axis=0)
        for g in range(N_KV_HEADS):
            lanes_g = slice(g * LANES, (g + 1) * LANES)
            kg, vg = k_band[:, lanes_g], v_band[:, lanes_g]
            zero = jnp.zeros_like(vg)
            rhs_pv = jnp.concatenate(
                [jnp.concatenate([jnp.where(low, vg, zero), ones_low], axis=1),
                 jnp.concatenate([jnp.where(low, zero, vg), ones_high], axis=1)], axis=0)
            lhs = jnp.concatenate(
                [q_ref[rows, (Q_PER_KV * g + i) * LANES:(Q_PER_KV * g + i + 1) * LANES]
                 for i in range(Q_PER_KV)], axis=0)
            s = lax.dot_general(lhs, kg, (((1,), (1,)), ((), ())), preferred_element_type=F32)
            probs, sink_terms = [], []
            for i in range(Q_PER_KV):
                sink = sinks_ref[g * Q_PER_KV + i] * LOG2E
                si = jnp.where(mask, s[i * WINDOW:(i + 1) * WINDOW, :], NEG)
                m = jnp.maximum(jnp.max(si, axis=-1, keepdims=True), sink)
                probs.append(jnp.exp2(si - m).astype(BF16))
                sink_terms.append(jnp.exp2(sink - m))
            for pair in range(2):
                p2 = jnp.concatenate(probs[2 * pair:2 * pair + 2], axis=1)
                r = jnp.dot(p2, rhs_pv, preferred_element_type=F32)
                den = r[:, LANES:] + jnp.where(low, sink_terms[2 * pair], sink_terms[2 * pair + 1])
                out_lanes = slice((2 * g + pair) * LANES, (2 * g + pair + 1) * LANES)
                o_ref[rows, out_lanes] = (r[:, :LANES] / den).astype(o_ref.dtype)


def _sgu_norm(v_ref, lng_ref, lnb_ref):
    v = v_ref[...].astype(F32)
    mu = jnp.mean(v, axis=-1, keepdims=True)
    d = v - mu
    var = jnp.mean(d * d, axis=-1, keepdims=True)
    return (d * lax.rsqrt(var + EPS) * lng_ref[...] + lnb_ref[...]).astype(BF16)


def _sgu_group(g, vn, u_ref, w_ref, b_ref, o_ref):
    ti = lax.broadcasted_iota(jnp.int32, (CHUNK, CHUNK), 0)
    si = lax.broadcasted_iota(jnp.int32, (CHUNK, CHUNK), 1)
    cols = slice(g * SGU_GROUP_DIM, (g + 1) * SGU_GROUP_DIM)
    w = jnp.where(si <= ti, w_ref[g], jnp.zeros_like(w_ref[g]))
    bias = b_ref[g]
    for c in range(u_ref.shape[0] // CHUNK):
        rows = slice(c * CHUNK, (c + 1) * CHUNK)
        sv = jnp.dot(w, vn[rows, cols], preferred_element_type=F32) + bias
        o_ref[rows, cols] = (u_ref[rows, cols].astype(F32) * sv).astype(o_ref.dtype)


MIX_TM = 256
MRG_TN = 512


def _mixer_kernel(blocks_per_seq, sinks_ref, q_ref, kc_ref, vc_ref, kp_ref, vp_ref, u_ref, v_ref,
                  lng_ref, lnb_ref, sw_ref, sb_ref, g_ref, h_ref, wa_ref, ws_ref, wo_ref, g2_ref, *rest):
    if len(rest) == 6:
        h1_ref, hn_ref, ya_ref, ys_ref, part_ref, merged_ref = rest
    else:
        next_in_ref, h1_ref, hn_ref, *seg_refs, ya_ref, ys_ref, part_ref, merged_ref = rest
        _cast_in_segments(next_in_ref, seg_refs)
    seq_start = (pl.program_id(0) * (MIX_TM // WINDOW)) % blocks_per_seq == 0
    _attention_rows(seq_start, sinks_ref, q_ref, kc_ref, vc_ref, kp_ref, vp_ref, ya_ref)
    ya = ya_ref[...]
    vn = _sgu_norm(v_ref, lng_ref, lnb_ref)
    groups_per_tile = SGU_GROUPS // (D_MODEL // MRG_TN)
    for j in range(D_MODEL // MRG_TN):
        cols = slice(j * MRG_TN, (j + 1) * MRG_TN)
        part_ref[:, cols] = g_ref[:, cols].astype(F32) * jnp.dot(
            ya, wa_ref[:, cols], preferred_element_type=F32)
        for g in range(j * groups_per_tile, (j + 1) * groups_per_tile):
            _sgu_group(g, vn, u_ref, sw_ref, sb_ref, ys_ref)
    ys = ys_ref[...]
    for j in range(D_MODEL // MRG_TN):
        cols = slice(j * MRG_TN, (j + 1) * MRG_TN)
        gs_cols = slice(D_MODEL + j * MRG_TN, D_MODEL + (j + 1) * MRG_TN)
        s = jnp.dot(ys, ws_ref[:, cols], preferred_element_type=F32)
        merged_ref[:, cols] = (g_ref[:, gs_cols].astype(F32) * s + part_ref[:, cols]).astype(BF16)
    merged = merged_ref[...]
    for j in range(D_MODEL // MRG_TN):
        cols = slice(j * MRG_TN, (j + 1) * MRG_TN)
        h1_ref[:, cols] = h_ref[:, cols] + jnp.dot(merged, wo_ref[:, cols], preferred_element_type=F32)
    hn_ref[...] = _rmsnorm_rows(h1_ref[...], g2_ref[...]).astype(hn_ref.dtype)


def _mixer(q, kv, z, gates, h, sinks, ln_g, ln_b, sgu_w, sgu_b, w_ab, w_sb, w_out, g2, seq_len,
           cast_in=None):
    T, tm = h.shape[0], MIX_TM
    n_rows = T // tm
    row = lambda width, col=0: pl.BlockSpec((tm, width), lambda i: (i, col))
    prev = lambda col: pl.BlockSpec((WINDOW, KVDUP_WIDTH),
                                    lambda i: (jnp.maximum(i * (tm // WINDOW) - 1, 0), col))
    const = lambda shape: pl.BlockSpec(shape, lambda i: (0,) * len(shape))
    resident = lambda shape: pl.BlockSpec((None,) + shape, lambda i: (0, 0, 0),
                                          pipeline_mode=pl.Buffered(1))
    in_specs = [pl.BlockSpec(memory_space=pltpu.SMEM),
                row(2 * ATTN_WIDTH), row(KVDUP_WIDTH, 0), row(KVDUP_WIDTH, 1), prev(0), prev(1),
                row(SGU_WIDTH, 0), row(SGU_WIDTH, 1), const((1, SGU_WIDTH)), const((1, SGU_WIDTH)),
                const((SGU_GROUPS, CHUNK, CHUNK)), const((SGU_GROUPS, CHUNK, 1)),
                row(2 * D_MODEL), row(D_MODEL),
                resident((ATTN_WIDTH, D_MODEL)), resident((SGU_WIDTH, D_MODEL)),
                resident((D_MODEL, D_MODEL)), resident((1, D_MODEL))]
    out_specs = [row(D_MODEL), row(D_MODEL)]
    out_shape = [jax.ShapeDtypeStruct((T, D_MODEL), F32), jax.ShapeDtypeStruct((T, D_MODEL), BF16)]
    args = [sinks, q, kv, kv, kv, kv, z, z, ln_g, ln_b, sgu_w, sgu_b, gates, h, w_ab, w_sb, w_out, g2]
    if cast_in is not None:
        w_spec, seg_specs, seg_shapes = _in_segment_specs(*cast_in, n_rows)
        in_specs.append(w_spec)
        args.append(cast_in[0])
        out_specs += seg_specs
        out_shape += seg_shapes
    return pl.pallas_call(
        functools.partial(_mixer_kernel, seq_len // WINDOW), grid=(n_rows,),
        in_specs=in_specs, out_specs=out_specs, out_shape=out_shape,
        scratch_shapes=[pltpu.VMEM((tm, ATTN_WIDTH), BF16), pltpu.VMEM((tm, SGU_WIDTH), BF16),
                        pltpu.VMEM((tm, D_MODEL), F32), pltpu.VMEM((tm, D_MODEL), BF16)],
        compiler_params=_params(1), name="token_mixer")(*args)


FFN_TM = 1024
FFN_TF = 512
FFN_SUB = 256
FFN_DOWN_TN = 512


def _ffn_kernel(last, hn_hbm, h1_hbm, wg_ref, wu_ref, wd_ref, gn_ref, *rest):
    if last:
        out_hbm, acc, act, load_sem, store_sem = rest
    else:
        h2_hbm, xn_hbm, acc, act, load_sem, store_sem = rest
    i, f = pl.program_id(0), pl.program_id(1)
    n_rows, nf = pl.num_programs(0), pl.num_programs(1)
    slot = i % 2

    def loads(tile, s):
        rows = pl.ds(pl.multiple_of(tile * FFN_TM, FFN_TM), FFN_TM)
        return (pltpu.make_async_copy(h1_hbm.at[rows, :], acc.at[s], load_sem.at[s, 0]),
                pltpu.make_async_copy(hn_hbm.at[rows, :], act.at[s], load_sem.at[s, 1]))

    def stores(tile, s):
        rows = pl.ds(pl.multiple_of(tile * FFN_TM, FFN_TM), FFN_TM)
        if last:
            return (pltpu.make_async_copy(acc.at[s], out_hbm.at[rows, :], store_sem.at[s, 0]),)
        return (pltpu.make_async_copy(acc.at[s], h2_hbm.at[rows, :], store_sem.at[s, 0]),
                pltpu.make_async_copy(act.at[s], xn_hbm.at[rows, :], store_sem.at[s, 1]))

    @pl.when((i == 0) & (f == 0))
    def _():
        for c in loads(0, 0):
            c.start()

    @pl.when(f == 0)
    def _():
        for c in loads(i, slot):
            c.wait()

    @pl.when((f == 1) & (i >= 1))
    def _():
        for c in stores(i - 1, 1 - slot):
            c.wait()

    @pl.when((f == 1) & (i + 1 < n_rows))
    def _():
        for c in loads(i + 1, 1 - slot):
            c.start()

    acc_s, act_s = acc.at[slot], act.at[slot]
    hn = act_s[...]
    acts = []
    for si in range(FFN_TF // FFN_SUB):
        sl = slice(si * FFN_SUB, (si + 1) * FFN_SUB)
        gate = jnp.dot(hn, wg_ref[:, sl], preferred_element_type=F32)
        up = jnp.dot(hn, wu_ref[:, sl], preferred_element_type=F32)
        acts.append((gate * (1.0 / (1.0 + jnp.exp(-gate))) * up).astype(BF16))
    a = jnp.concatenate(acts, axis=1)
    for ci in range(D_MODEL // FFN_DOWN_TN):
        cols = slice(ci * FFN_DOWN_TN, (ci + 1) * FFN_DOWN_TN)
        acc_s[:, cols] += jnp.dot(a, wd_ref[:, cols], preferred_element_type=F32)

    @pl.when(f == nf - 1)
    def _():
        y = _rmsnorm_rows(acc_s[...], gn_ref[...])
        if last:
            acc_s[...] = y
        else:
            act_s[...] = y.astype(BF16)
        for c in stores(i, slot):
            c.start()

    @pl.when((f == nf - 1) & (i == n_rows - 1))
    def _():
        for c in stores(i, slot):
            c.wait()


def _ffn(hn, h1, w_gu, w_down, g_next, last):
    T = h1.shape[0]
    nf = D_FF // FFN_TF
    hbm = pl.BlockSpec(memory_space=pl.ANY)
    rows_f32 = jax.ShapeDtypeStruct((T, D_MODEL), F32)
    return pl.pallas_call(
        functools.partial(_ffn_kernel, last), grid=(T // FFN_TM, nf),
        in_specs=[hbm, hbm,
                  pl.BlockSpec((None, D_MODEL, FFN_TF), lambda i, f: (0, 0, f)),
                  pl.BlockSpec((None, D_MODEL, FFN_TF), lambda i, f: (0, 0, nf + f)),
                  pl.BlockSpec((None, FFN_TF, D_MODEL), lambda i, f: (0, f, 0)),
                  pl.BlockSpec((1, D_MODEL), lambda i, f: (0, 0))],
        out_specs=hbm if last else [hbm, hbm],
        out_shape=rows_f32 if last else [rows_f32, jax.ShapeDtypeStruct((T, D_MODEL), BF16)],
        scratch_shapes=[pltpu.VMEM((2, FFN_TM, D_MODEL), F32), pltpu.VMEM((2, FFN_TM, D_MODEL), BF16),
                        pltpu.SemaphoreType.DMA((2, 2)), pltpu.SemaphoreType.DMA((2, 2))],
        compiler_params=_params(2), name="swiglu_ffn")(hn, h1, w_gu, w_gu, w_down, g_next)


def kernel(x, positions, norm1_g, w_in, b_in, sinks, sgu_ln_g, sgu_ln_b, sgu_w, sgu_b,
           w_attn_branch, w_sgu_branch, w_out, norm2_g, w_gate_up, w_down, final_g):
    B, S, D = x.shape
    T = B * S
    depth = w_in.shape[0]
    assert D == D_MODEL and S % MIX_TM == 0 and T % FFN_TM == 0 and w_in.shape[-1] == IN_WIDTH

    inv_freq = ROPE_THETA ** (-jnp.arange(0, ROPE_DIM, 2, dtype=F32) / ROPE_DIM)
    lane = jnp.arange(LANES) % HEAD_DIM
    invf_lanes = jnp.where(lane < ROPE_DIM, inv_freq[lane % ROPE_HALF], 0.0)[None, :]

    sgu_w_b, sgu_b_c = sgu_w.astype(BF16), sgu_b[..., None]

    h = x.reshape(T, D)
    xn, *tabs, w_qkv, w_z, w_g = _rmsnorm(h, norm1_g[0][None, :], positions.reshape(T, 1), invf_lanes, w_in)
    for l in range(depth):
        b_qkv, b_z, b_g = (b_in[l, lo:hi][None, None, :] for lo, hi in IN_SEGMENTS)
        gates, w_gu, w_dn = _proj(_gates_kernel, "proj_gates", xn, w_g, b_g, tn=2048,
                                  cast=((w_gate_up, w_down), l))
        z, w_ab, w_sb, w_o = _proj(_gelu_kernel, "proj_gelu", xn, w_z, b_z, tn=2048,
                                   cast=((w_attn_branch, w_sgu_branch, w_out), l))
        q, kv = _qkv_proj(xn, w_qkv, b_qkv, tabs)
        mix = functools.partial(_mixer, q, kv, z, gates, h, sinks[l], sgu_ln_g[l][None, :],
                                sgu_ln_b[l][None, :], sgu_w_b[l], sgu_b_c[l], w_ab, w_sb, w_o,
                                norm2_g[l][None, None, :], S)
        if l == depth - 1:
            h1, hn = mix()
            out = _ffn(hn, h1, w_gu, w_dn, final_g[None, :], last=True)
        else:
            h1, hn, w_qkv, w_z, w_g = mix(cast_in=(w_in, l + 1))
            h, xn = _ffn(hn, h1, w_gu, w_dn, norm1_g[l + 1][None, :], last=False)
    return out.reshape(B, S, D)
```

```python
import functools

import jax
import jax.numpy as jnp
from jax import lax
from jax.experimental import pallas as pl
from jax.experimental.pallas import tpu as pltpu

D_MODEL = 2048
N_Q_HEADS = 16
N_KV_HEADS = 4
HEAD_DIM = 64
Q_PER_KV = N_Q_HEADS // N_KV_HEADS
ATTN_WIDTH = N_Q_HEADS * HEAD_DIM
KV_WIDTH = N_KV_HEADS * HEAD_DIM
WINDOW = 128
ROPE_THETA = 500000.0
ROPE_DIM = HEAD_DIM // 4
ROPE_HALF = ROPE_DIM // 2
SGU_WIDTH = D_MODEL // 2
SGU_GROUPS = 8
SGU_GROUP_DIM = SGU_WIDTH // SGU_GROUPS
CHUNK = 128
D_FF = 5632
EPS = 1e-5
NEG = -1e30
LOG2E = 1.4426950408889634

OFF_Q = 0
OFF_K = OFF_Q + ATTN_WIDTH
OFF_Z = OFF_K + 2 * KV_WIDTH
OFF_G = OFF_Z + 2 * SGU_WIDTH
IN_WIDTH = OFF_G + 2 * D_MODEL
IN_SEGMENTS = ((OFF_Q, OFF_Z), (OFF_Z, OFF_G), (OFF_G, IN_WIDTH))
KVDUP_WIDTH = 2 * KV_WIDTH

LANES = 128
VMEM_LIMIT = 52 * 1024 * 1024

BF16 = jnp.bfloat16
F32 = jnp.float32


def _params(n_axes):
    return pltpu.CompilerParams(dimension_semantics=("arbitrary",) * n_axes,
                                vmem_limit_bytes=VMEM_LIMIT)


def _rmsnorm_rows(x, g):
    return x * lax.rsqrt(jnp.mean(x * x, axis=-1, keepdims=True) + EPS) * g


def _low_half_lanes(shape=(1, LANES)):
    return lax.broadcasted_iota(jnp.int32, shape, len(shape) - 1) % LANES < HEAD_DIM


def _rope_tables(pos_ref, invf_ref, c_ref, s1_ref, s2_ref):
    lane = lax.broadcasted_iota(jnp.int32, c_ref.shape, 1) % HEAD_DIM
    ang = pos_ref[...].astype(F32) * invf_ref[...]
    cos, sin = jnp.cos(ang), jnp.sin(ang)
    first, second = lane < ROPE_HALF, (lane >= ROPE_HALF) & (lane < ROPE_DIM)
    c_ref[...] = jnp.where(lane < ROPE_DIM, cos, 1.0)
    s1_ref[...] = jnp.where(first, -sin, 0.0)
    s2_ref[...] = jnp.where(second, sin, 0.0)


def _cast_in_segments(w32_ref, seg_refs):
    for (lo, hi), seg_ref in zip(IN_SEGMENTS, seg_refs):
        seg_ref[...] = w32_ref[:, lo:hi].astype(BF16)


def _in_segment_specs(w_in32, layer, n_steps):
    rows = D_MODEL // n_steps
    in_spec = pl.BlockSpec((None, rows, IN_WIDTH), lambda i: (layer, i, 0))
    out_specs = [pl.BlockSpec((None, rows, hi - lo), lambda i: (0, i, 0)) for lo, hi in IN_SEGMENTS]
    out_shape = [jax.ShapeDtypeStruct((1, D_MODEL, hi - lo), BF16) for lo, hi in IN_SEGMENTS]
    return in_spec, out_specs, out_shape


def _norm_kernel(x_ref, g_ref, pos_ref, invf_ref, w32_ref, o_ref, c_ref, s1_ref, s2_ref, *seg_refs):
    _rope_tables(pos_ref, invf_ref, c_ref, s1_ref, s2_ref)
    _cast_in_segments(w32_ref, seg_refs)
    o_ref[...] = _rmsnorm_rows(x_ref[...], g_ref[...]).astype(o_ref.dtype)


def _rmsnorm(x, g, pos, invf_lanes, w_in32, tm=512):
    T, D = x.shape
    w_spec, seg_specs, seg_shapes = _in_segment_specs(w_in32, 0, T // tm)
    tab_spec = pl.BlockSpec((tm, LANES), lambda i: (i, 0))
    tab_shape = jax.ShapeDtypeStruct((T, LANES), F32)
    return pl.pallas_call(
        _norm_kernel, grid=(T // tm,),
        in_specs=[pl.BlockSpec((tm, D), lambda i: (i, 0)), pl.BlockSpec((1, D), lambda i: (0, 0)),
                  pl.BlockSpec((tm, 1), lambda i: (i, 0)), pl.BlockSpec((1, LANES), lambda i: (0, 0)),
                  w_spec],
        out_specs=[pl.BlockSpec((tm, D), lambda i: (i, 0))] + [tab_spec] * 3 + seg_specs,
        out_shape=[jax.ShapeDtypeStruct((T, D), BF16)] + [tab_shape] * 3 + seg_shapes,
        compiler_params=_params(1), name="rmsnorm0")(x, g, pos, invf_lanes, w_in32)


PROJ_TM = 1024
PROJ_SUB_M = 128
PROJ_SUB = 256


def _rope(r, c, s1, s2):
    return (r * c + pltpu.roll(r, LANES - ROPE_HALF, 1) * s1
            + pltpu.roll(r, ROPE_HALF, 1) * s2)


def _sub_tiles(xn_ref, w_ref, b_ref):
    for mi in range(xn_ref.shape[0] // PROJ_SUB_M):
        rows = slice(mi * PROJ_SUB_M, (mi + 1) * PROJ_SUB_M)
        xn = xn_ref[rows, :]
        for si in range(w_ref.shape[1] // PROJ_SUB):
            sl = slice(si * PROJ_SUB, (si + 1) * PROJ_SUB)
            yield rows, si, jnp.dot(xn, w_ref[:, sl], preferred_element_type=F32) + b_ref[:, sl]


def _cast_blocks(rest):
    n = len(rest) // 2
    for src, dst in zip(rest[:n], rest[n + 1:]):
        dst[...] = src[...].astype(BF16)
    return rest[n]


def _gates_kernel(xn_ref, w_ref, b_ref, *rest):
    o_ref = _cast_blocks(rest)
    for rows, si, r in _sub_tiles(xn_ref, w_ref, b_ref):
        o_ref[rows, si * PROJ_SUB:(si + 1) * PROJ_SUB] = (
            1.0 / (1.0 + jnp.exp2(r * -LOG2E))).astype(o_ref.dtype)


def _gelu_kernel(xn_ref, w_ref, b_ref, *rest):
    o_ref = _cast_blocks(rest)
    for rows, si, r in _sub_tiles(xn_ref, w_ref, b_ref):
        o_ref[rows, si * PROJ_SUB:(si + 1) * PROJ_SUB] = (
            0.5 * r * (1.0 + lax.erf(r * (2.0 ** -0.5)))).astype(o_ref.dtype)


def _qkv_kernel(xn_ref, w_ref, b_ref, c_ref, s1_ref, s2_ref, q_ref, kv_ref):
    low = _low_half_lanes()
    for rows, si, r in _sub_tiles(xn_ref, w_ref, b_ref):
        c, s1, s2 = c_ref[rows, :], s1_ref[rows, :], s2_ref[rows, :]
        for gi in range(PROJ_SUB // LANES):
            x = r[:, gi * LANES:(gi + 1) * LANES]
            col = si * PROJ_SUB + gi * LANES
            if col < ATTN_WIDTH:
                xq = _rope(x, c, s1, s2) * (HEAD_DIM ** -0.5 * LOG2E)
                q_ref[rows, 2 * col:2 * col + LANES] = jnp.where(low, xq, 0.0).astype(q_ref.dtype)
                q_ref[rows, 2 * col + LANES:2 * col + 2 * LANES] = jnp.where(low, 0.0, xq).astype(q_ref.dtype)
                continue
            if col < ATTN_WIDTH + KV_WIDTH:
                x = _rope(x, c, s1, s2)
            swapped = pltpu.roll(x, HEAD_DIM, 1)
            lo = 2 * (col - ATTN_WIDTH)
            kv_ref[rows, lo:lo + LANES] = jnp.where(low, x, swapped).astype(kv_ref.dtype)
            kv_ref[rows, lo + LANES:lo + 2 * LANES] = jnp.where(low, swapped, x).astype(kv_ref.dtype)


def _qkv_proj(xn, w, b, tabs):
    T, tm = xn.shape[0], PROJ_TM
    width = w.shape[-1]
    tab_spec = pl.BlockSpec((tm, LANES), lambda i: (i, 0))
    return pl.pallas_call(
        _qkv_kernel, grid=(T // tm,),
        in_specs=[pl.BlockSpec((tm, D_MODEL), lambda i: (i, 0)),
                  pl.BlockSpec((None, D_MODEL, width), lambda i: (0, 0, 0)),
                  pl.BlockSpec((None, 1, width), lambda i: (0, 0, 0))] + [tab_spec] * 3,
        out_specs=[pl.BlockSpec((tm, 2 * ATTN_WIDTH), lambda i: (i, 0)),
                   pl.BlockSpec((tm, 2 * KVDUP_WIDTH), lambda i: (i, 0))],
        out_shape=[jax.ShapeDtypeStruct((T, 2 * ATTN_WIDTH), BF16),
                   jax.ShapeDtypeStruct((T, 2 * KVDUP_WIDTH), BF16)],
        compiler_params=_params(1), name="proj_qkv")(xn, w, b, *tabs)


def _proj(body, name, xn, w, b, tn, cast):
    T, tm = xn.shape[0], PROJ_TM
    width = w.shape[-1]
    n_i, n_j = T // tm, width // tn
    in_specs = [pl.BlockSpec((tm, D_MODEL), lambda i, j: (i, 0)),
                pl.BlockSpec((None, D_MODEL, tn), lambda i, j: (0, 0, j)),
                pl.BlockSpec((None, 1, tn), lambda i, j: (0, 0, j))]
    out_specs = [pl.BlockSpec((tm, tn), lambda i, j: (i, j))]
    out_shape = [jax.ShapeDtypeStruct((T, width), BF16)]
    args = [xn, w, b]
    arrays, layer = cast
    for a in arrays:
        blk = (None, a.shape[1] // n_i, a.shape[2] // n_j)
        in_specs.append(pl.BlockSpec(blk, lambda i, j: (layer, i, j)))
        out_specs.append(pl.BlockSpec(blk, lambda i, j: (0, i, j)))
        out_shape.append(jax.ShapeDtypeStruct((1,) + a.shape[1:], BF16))
        args.append(a)
    return pl.pallas_call(
        body, grid=(n_i, n_j), in_specs=in_specs, out_specs=out_specs, out_shape=out_shape,
        compiler_params=_params(2), name=name)(*args)


def _attention_rows(seq_start, sinks_ref, q_ref, kc_ref, vc_ref, kp_ref, vp_ref, o_ref):
    qi = lax.broadcasted_iota(jnp.int32, (WINDOW, 2 * WINDOW), 0)
    kj = lax.broadcasted_iota(jnp.int32, (WINDOW, 2 * WINDOW), 1)
    rel = qi + WINDOW - kj
    band_ok = (rel >= 0) & (rel < WINDOW)
    low = _low_half_lanes()
    ones_low = jnp.broadcast_to(jnp.where(low, 1.0, 0.0).astype(BF16), (2 * WINDOW, LANES))
    ones_high = jnp.broadcast_to(jnp.where(low, 0.0, 1.0).astype(BF16), (2 * WINDOW, LANES))
    for c in range(q_ref.shape[0] // WINDOW):
        rows = slice(c * WINDOW, (c + 1) * WINDOW)
        if c == 0:
            first_key = jnp.where(seq_start, WINDOW, 0)
            mask = band_ok & (kj >= first_key)
            k_prev, v_prev = kp_ref[...], vp_ref[...]
        else:
            mask = band_ok
            prev_rows = slice((c - 1) * WINDOW, c * WINDOW)
            k_prev, v_prev = kc_ref[prev_rows, :], vc_ref[prev_rows, :]
        k_band = jnp.concatenate([k_prev, kc_ref[rows, :]], axis=0)
        v_band = jnp.concatenate([v_prev, vc_ref[rows, :]], axis=0)
        for g in range(N_KV_HEADS):
            lanes_g = slice(g * LANES, (g + 1) * LANES)
            kg, vg = k_band[:, lanes_g], v_band[:, lanes_g]
            zero = jnp.zeros_like(vg)
            rhs_pv = jnp.concatenate(
                [jnp.concatenate([jnp.where(low, vg, zero), ones_low], axis=1),
                 jnp.concatenate([jnp.where(low, zero, vg), ones_high], axis=1)], axis=0)
            lhs = jnp.concatenate(
                [q_ref[rows, (Q_PER_KV * g + i) * LANES:(Q_PER_KV * g + i + 1) * LANES]
                 for i in range(Q_PER_KV)], axis=0)
            s = lax.dot_general(lhs, kg, (((1,), (1,)), ((), ())), preferred_element_type=F32)
            probs, sink_terms = [], []
            for i in range(Q_PER_KV):
                sink = sinks_ref[g * Q_PER_KV + i] * LOG2E
                si = jnp.where(mask, s[i * WINDOW:(i + 1) * WINDOW, :], NEG)
                m = jnp.maximum(jnp.max(si, axis=-1, keepdims=True), sink)
                probs.append(jnp.exp2(si - m).astype(BF16))
                sink_terms.append(jnp.exp2(sink - m))
            for pair in range(2):
                p2 = jnp.concatenate(probs[2 * pair:2 * pair + 2], axis=1)
                r = jnp.dot(p2, rhs_pv, preferred_element_type=F32)
                den = r[:, LANES:] + jnp.where(low, sink_terms[2 * pair], sink_terms[2 * pair + 1])
                out_lanes = slice((2 * g + pair) * LANES, (2 * g + pair + 1) * LANES)
                o_ref[rows, out_lanes] = (r[:, :LANES] / den).astype(o_ref.dtype)


def _sgu_norm(v_ref, lng_ref, lnb_ref):
    v = v_ref[...].astype(F32)
    mu = jnp.mean(v, axis=-1, keepdims=True)
    d = v - mu
    var = jnp.mean(d * d, axis=-1, keepdims=True)
    return (d * lax.rsqrt(var + EPS) * lng_ref[...] + lnb_ref[...]).astype(BF16)


def _sgu_group(g, vn, u_ref, w_ref, b_ref, o_ref):
    ti = lax.broadcasted_iota(jnp.int32, (CHUNK, CHUNK), 0)
    si = lax.broadcasted_iota(jnp.int32, (CHUNK, CHUNK), 1)
    cols = slice(g * SGU_GROUP_DIM, (g + 1) * SGU_GROUP_DIM)
    w = jnp.where(si <= ti, w_ref[g], jnp.zeros_like(w_ref[g]))
    bias = b_ref[g]
    for c in range(u_ref.shape[0] // CHUNK):
        rows = slice(c * CHUNK, (c + 1) * CHUNK)
        sv = jnp.dot(w, vn[rows, cols], preferred_element_type=F32) + bias
        o_ref[rows, cols] = (u_ref[rows, cols].astype(F32) * sv).astype(o_ref.dtype)


MIX_TM = 256
MRG_TN = 512


def _mixer_kernel(blocks_per_seq, sinks_ref, q_ref, kc_ref, vc_ref, kp_ref, vp_ref, u_ref, v_ref,
                  lng_ref, lnb_ref, sw_ref, sb_ref, g_ref, h_ref, wa_ref, ws_ref, wo_ref, g2_ref, *rest):
    if len(rest) == 7:
        h1_ref, hn_ref, scale_ref, ya_ref, ys_ref, part_ref, merged_ref = rest
    else:
        next_in_ref, h1_ref, hn_ref, scale_ref, *seg_refs, ya_ref, ys_ref, part_ref, merged_ref = rest
        _cast_in_segments(next_in_ref, seg_refs)
    seq_start = (pl.program_id(0) * (MIX_TM // WINDOW)) % blocks_per_seq == 0
    _attention_rows(seq_start, sinks_ref, q_ref, kc_ref, vc_ref, kp_ref, vp_ref, ya_ref)
    ya = ya_ref[...]
    vn = _sgu_norm(v_ref, lng_ref, lnb_ref)
    groups_per_tile = SGU_GROUPS // (D_MODEL // MRG_TN)
    for j in range(D_MODEL // MRG_TN):
        cols = slice(j * MRG_TN, (j + 1) * MRG_TN)
        part_ref[:, cols] = g_ref[:, cols].astype(F32) * jnp.dot(
            ya, wa_ref[:, cols], preferred_element_type=F32)
        for g in range(j * groups_per_tile, (j + 1) * groups_per_tile):
            _sgu_group(g, vn, u_ref, sw_ref, sb_ref, ys_ref)
    ys = ys_ref[...]
    for j in range(D_MODEL // MRG_TN):
        cols = slice(j * MRG_TN, (j + 1) * MRG_TN)
        gs_cols = slice(D_MODEL + j * MRG_TN, D_MODEL + (j + 1) * MRG_TN)
        s = jnp.dot(ys, ws_ref[:, cols], preferred_element_type=F32)
        merged_ref[:, cols] = (g_ref[:, gs_cols].astype(F32) * s + part_ref[:, cols]).astype(BF16)
    merged = merged_ref[...]
    sum_sq = jnp.zeros((merged.shape[0], 1), F32)
    for j in range(D_MODEL // MRG_TN):
        cols = slice(j * MRG_TN, (j + 1) * MRG_TN)
        h1 = h_ref[:, cols] + jnp.dot(merged, wo_ref[:, cols], preferred_element_type=F32)
        h1_ref[:, cols] = h1
        hn_ref[:, cols] = (h1 * g2_ref[:, cols]).astype(hn_ref.dtype)
        sum_sq += jnp.sum(h1 * h1, axis=-1, keepdims=True)
    scale = lax.rsqrt(sum_sq * (1.0 / D_MODEL) + EPS)
    scale_ref[...] = jnp.broadcast_to(scale, scale_ref.shape)


def _mixer(q, kv, z, gates, h, sinks, ln_g, ln_b, sgu_w, sgu_b, w_ab, w_sb, w_out, g2, seq_len,
           cast_in=None):
    T, tm = h.shape[0], MIX_TM
    n_rows = T // tm
    row = lambda width, col=0: pl.BlockSpec((tm, width), lambda i: (i, col))
    prev = lambda col: pl.BlockSpec((WINDOW, KVDUP_WIDTH),
                                    lambda i: (jnp.maximum(i * (tm // WINDOW) - 1, 0), col))
    const = lambda shape: pl.BlockSpec(shape, lambda i: (0,) * len(shape))
    resident = lambda shape: pl.BlockSpec((None,) + shape, lambda i: (0, 0, 0),
                                          pipeline_mode=pl.Buffered(1))
    in_specs = [pl.BlockSpec(memory_space=pltpu.SMEM),
                row(2 * ATTN_WIDTH), row(KVDUP_WIDTH, 0), row(KVDUP_WIDTH, 1), prev(0), prev(1),
                row(SGU_WIDTH, 0), row(SGU_WIDTH, 1), const((1, SGU_WIDTH)), const((1, SGU_WIDTH)),
                const((SGU_GROUPS, CHUNK, CHUNK)), const((SGU_GROUPS, CHUNK, 1)),
                row(2 * D_MODEL), row(D_MODEL),
                resident((ATTN_WIDTH, D_MODEL)), resident((SGU_WIDTH, D_MODEL)),
                resident((D_MODEL, D_MODEL)), resident((1, D_MODEL))]
    out_specs = [row(D_MODEL), row(D_MODEL), row(LANES)]
    out_shape = [jax.ShapeDtypeStruct((T, D_MODEL), F32), jax.ShapeDtypeStruct((T, D_MODEL), BF16),
                 jax.ShapeDtypeStruct((T, LANES), F32)]
    args = [sinks, q, kv, kv, kv, kv, z, z, ln_g, ln_b, sgu_w, sgu_b, gates, h, w_ab, w_sb, w_out, g2]
    if cast_in is not None:
        w_spec, seg_specs, seg_shapes = _in_segment_specs(*cast_in, n_rows)
        in_specs.append(w_spec)
        args.append(cast_in[0])
        out_specs += seg_specs
        out_shape += seg_shapes
    return pl.pallas_call(
        functools.partial(_mixer_kernel, seq_len // WINDOW), grid=(n_rows,),
        in_specs=in_specs, out_specs=out_specs, out_shape=out_shape,
        scratch_shapes=[pltpu.VMEM((tm, ATTN_WIDTH), BF16), pltpu.VMEM((tm, SGU_WIDTH), BF16),
                        pltpu.VMEM((tm, D_MODEL), F32), pltpu.VMEM((tm, D_MODEL), BF16)],
        compiler_params=_params(1), name="token_mixer")(*args)


FFN_TM = 1024
FFN_TF = 512
FFN_SUB = 256
FFN_DOWN_TN = 512


def _ffn_kernel(last, hn_hbm, h1_hbm, scale_ref, wg_ref, wu_ref, wd_ref, gn_ref, *rest):
    if last:
        out_hbm, acc, act, load_sem, store_sem = rest
    else:
        h2_hbm, xn_hbm, acc, act, load_sem, store_sem = rest
    i, f = pl.program_id(0), pl.program_id(1)
    n_rows, nf = pl.num_programs(0), pl.num_programs(1)
    slot = i % 2

    def loads(tile, s):
        rows = pl.ds(pl.multiple_of(tile * FFN_TM, FFN_TM), FFN_TM)
        return (pltpu.make_async_copy(h1_hbm.at[rows, :], acc.at[s], load_sem.at[s, 0]),
                pltpu.make_async_copy(hn_hbm.at[rows, :], act.at[s], load_sem.at[s, 1]))

    def stores(tile, s):
        rows = pl.ds(pl.multiple_of(tile * FFN_TM, FFN_TM), FFN_TM)
        if last:
            return (pltpu.make_async_copy(acc.at[s], out_hbm.at[rows, :], store_sem.at[s, 0]),)
        return (pltpu.make_async_copy(acc.at[s], h2_hbm.at[rows, :], store_sem.at[s, 0]),
                pltpu.make_async_copy(act.at[s], xn_hbm.at[rows, :], store_sem.at[s, 1]))

    @pl.when((i == 0) & (f == 0))
    def _():
        for c in loads(0, 0):
            c.start()

    @pl.when(f == 0)
    def _():
        for c in loads(i, slot):
            c.wait()

    @pl.when((f == 1) & (i >= 1))
    def _():
        for c in stores(i - 1, 1 - slot):
            c.wait()

    @pl.when((f == 1) & (i + 1 < n_rows))
    def _():
        for c in loads(i + 1, 1 - slot):
            c.start()

    acc_s, act_s = acc.at[slot], act.at[slot]
    hn = act_s[...]
    row_scale = jnp.concatenate([scale_ref[...]] * (FFN_SUB // LANES), axis=1)
    acts = []
    for si in range(FFN_TF // FFN_SUB):
        sl = slice(si * FFN_SUB, (si + 1) * FFN_SUB)
        gate = jnp.dot(hn, wg_ref[:, sl], preferred_element_type=F32) * row_scale
        up = jnp.dot(hn, wu_ref[:, sl], preferred_element_type=F32) * row_scale
        acts.append((gate * (1.0 / (1.0 + jnp.exp2(gate * -LOG2E))) * up).astype(BF16))
    a = jnp.concatenate(acts, axis=1)
    for ci in range(D_MODEL // FFN_DOWN_TN):
        cols = slice(ci * FFN_DOWN_TN, (ci + 1) * FFN_DOWN_TN)
        acc_s[:, cols] += jnp.dot(a, wd_ref[:, cols], preferred_element_type=F32)

    @pl.when(f == nf - 1)
    def _():
        y = _rmsnorm_rows(acc_s[...], gn_ref[...])
        if last:
            acc_s[...] = y
        else:
            act_s[...] = y.astype(BF16)
        for c in stores(i, slot):
            c.start()

    @pl.when((f == nf - 1) & (i == n_rows - 1))
    def _():
        for c in stores(i, slot):
            c.wait()


def _ffn(hn, h1, scale, w_gu, w_down, g_next, last):
    T = h1.shape[0]
    nf = D_FF // FFN_TF
    hbm = pl.BlockSpec(memory_space=pl.ANY)
    rows_f32 = jax.ShapeDtypeStruct((T, D_MODEL), F32)
    return pl.pallas_call(
        functools.partial(_ffn_kernel, last), grid=(T // FFN_TM, nf),
        in_specs=[hbm, hbm, pl.BlockSpec((FFN_TM, LANES), lambda i, f: (i, 0)),
                  pl.BlockSpec((None, D_MODEL, FFN_TF), lambda i, f: (0, 0, f)),
                  pl.BlockSpec((None, D_MODEL, FFN_TF), lambda i, f: (0, 0, nf + f)),
                  pl.BlockSpec((None, FFN_TF, D_MODEL), lambda i, f: (0, f, 0)),
                  pl.BlockSpec((1, D_MODEL), lambda i, f: (0, 0))],
        out_specs=hbm if last else [hbm, hbm],
        out_shape=rows_f32 if last else [rows_f32, jax.ShapeDtypeStruct((T, D_MODEL), BF16)],
        scratch_shapes=[pltpu.VMEM((2, FFN_TM, D_MODEL), F32), pltpu.VMEM((2, FFN_TM, D_MODEL), BF16),
                        pltpu.SemaphoreType.DMA((2, 2)), pltpu.SemaphoreType.DMA((2, 2))],
        compiler_params=_params(2), name="swiglu_ffn")(hn, h1, scale, w_gu, w_gu, w_down, g_next)


def kernel(x, positions, norm1_g, w_in, b_in, sinks, sgu_ln_g, sgu_ln_b, sgu_w, sgu_b,
           w_attn_branch, w_sgu_branch, w_out, norm2_g, w_gate_up, w_down, final_g):
    B, S, D = x.shape
    T = B * S
    depth = w_in.shape[0]
    assert D == D_MODEL and S % MIX_TM == 0 and T % FFN_TM == 0 and w_in.shape[-1] == IN_WIDTH

    inv_freq = ROPE_THETA ** (-jnp.arange(0, ROPE_DIM, 2, dtype=F32) / ROPE_DIM)
    lane = jnp.arange(LANES) % HEAD_DIM
    invf_lanes = jnp.where(lane < ROPE_DIM, inv_freq[lane % ROPE_HALF], 0.0)[None, :]

    sgu_w_b, sgu_b_c = sgu_w.astype(BF16), sgu_b[..., None]

    h = x.reshape(T, D)
    xn, *tabs, w_qkv, w_z, w_g = _rmsnorm(h, norm1_g[0][None, :], positions.reshape(T, 1), invf_lanes, w_in)
    for l in range(depth):
        b_qkv, b_z, b_g = (b_in[l, lo:hi][None, None, :] for lo, hi in IN_SEGMENTS)
        gates, w_gu, w_dn = _proj(_gates_kernel, "proj_gates", xn, w_g, b_g, tn=2048,
                                  cast=((w_gate_up, w_down), l))
        z, w_ab, w_sb, w_o = _proj(_gelu_kernel, "proj_gelu", xn, w_z, b_z, tn=2048,
                                   cast=((w_attn_branch, w_sgu_branch, w_out), l))
        q, kv = _qkv_proj(xn, w_qkv, b_qkv, tabs)
        mix = functools.partial(_mixer, q, kv, z, gates, h, sinks[l], sgu_ln_g[l][None, :],
                                sgu_ln_b[l][None, :], sgu_w_b[l], sgu_b_c[l], w_ab, w_sb, w_o,
                                norm2_g[l][None, None, :], S)
        if l == depth - 1:
            h1, hn, hn_scale = mix()
            out = _ffn(hn, h1, hn_scale, w_gu, w_dn, final_g[None, :], last=True)
        else:
            h1, hn, hn_scale, w_qkv, w_z, w_g = mix(cast_in=(w_in, l + 1))
            h, xn = _ffn(hn, h1, hn_scale, w_gu, w_dn, norm1_g[l + 1][None, :], last=False)
    return out.reshape(B, S, D)
```

```python
import functools

import jax
import jax.numpy as jnp
from jax import lax
from jax.experimental import pallas as pl
from jax.experimental.pallas import tpu as pltpu

D_MODEL = 2048
N_Q_HEADS = 16
N_KV_HEADS = 4
HEAD_DIM = 64
Q_PER_KV = N_Q_HEADS // N_KV_HEADS
ATTN_WIDTH = N_Q_HEADS * HEAD_DIM
KV_WIDTH = N_KV_HEADS * HEAD_DIM
WINDOW = 128
ROPE_THETA = 500000.0
ROPE_DIM = HEAD_DIM // 4
ROPE_HALF = ROPE_DIM // 2
SGU_WIDTH = D_MODEL // 2
SGU_GROUPS = 8
SGU_GROUP_DIM = SGU_WIDTH // SGU_GROUPS
CHUNK = 128
D_FF = 5632
EPS = 1e-5
NEG = -1e30
LOG2E = 1.4426950408889634

OFF_Q = 0
OFF_K = OFF_Q + ATTN_WIDTH
OFF_Z = OFF_K + 2 * KV_WIDTH
OFF_G = OFF_Z + 2 * SGU_WIDTH
IN_WIDTH = OFF_G + 2 * D_MODEL
IN_SEGMENTS = ((OFF_Q, OFF_Z), (OFF_Z, OFF_G), (OFF_G, IN_WIDTH))
KVDUP_WIDTH = 2 * KV_WIDTH

LANES = 128
VMEM_LIMIT = 52 * 1024 * 1024

BF16 = jnp.bfloat16
F32 = jnp.float32


def _params(n_axes):
    return pltpu.CompilerParams(dimension_semantics=("arbitrary",) * n_axes,
                                vmem_limit_bytes=VMEM_LIMIT)


def _rmsnorm_rows(x, g):
    return x * lax.rsqrt(jnp.mean(x * x, axis=-1, keepdims=True) + EPS) * g


def _low_half_lanes(shape=(1, LANES)):
    return lax.broadcasted_iota(jnp.int32, shape, len(shape) - 1) % LANES < HEAD_DIM


def _rope_tables(pos_ref, invf_ref, c_ref, s1_ref, s2_ref):
    lane = lax.broadcasted_iota(jnp.int32, c_ref.shape, 1) % HEAD_DIM
    ang = pos_ref[...].astype(F32) * invf_ref[...]
    cos, sin = jnp.cos(ang), jnp.sin(ang)
    first, second = lane < ROPE_HALF, (lane >= ROPE_HALF) & (lane < ROPE_DIM)
    c_ref[...] = jnp.where(lane < ROPE_DIM, cos, 1.0)
    s1_ref[...] = jnp.where(first, -sin, 0.0)
    s2_ref[...] = jnp.where(second, sin, 0.0)


def _cast_in_segments(w32_ref, seg_refs):
    for (lo, hi), seg_ref in zip(IN_SEGMENTS, seg_refs):
        seg_ref[...] = w32_ref[:, lo:hi].astype(BF16)


def _in_segment_specs(w_in32, layer, n_steps):
    rows = D_MODEL // n_steps
    in_spec = pl.BlockSpec((None, rows, IN_WIDTH), lambda i: (layer, i, 0))
    out_specs = [pl.BlockSpec((None, rows, hi - lo), lambda i: (0, i, 0)) for lo, hi in IN_SEGMENTS]
    out_shape = [jax.ShapeDtypeStruct((1, D_MODEL, hi - lo), BF16) for lo, hi in IN_SEGMENTS]
    return in_spec, out_specs, out_shape


def _norm_kernel(x_ref, g_ref, pos_ref, invf_ref, w32_ref, o_ref, c_ref, s1_ref, s2_ref, *seg_refs):
    _rope_tables(pos_ref, invf_ref, c_ref, s1_ref, s2_ref)
    _cast_in_segments(w32_ref, seg_refs)
    o_ref[...] = _rmsnorm_rows(x_ref[...], g_ref[...]).astype(o_ref.dtype)


def _rmsnorm(x, g, pos, invf_lanes, w_in32, tm=512):
    T, D = x.shape
    w_spec, seg_specs, seg_shapes = _in_segment_specs(w_in32, 0, T // tm)
    tab_spec = pl.BlockSpec((tm, LANES), lambda i: (i, 0))
    tab_shape = jax.ShapeDtypeStruct((T, LANES), F32)
    return pl.pallas_call(
        _norm_kernel, grid=(T // tm,),
        in_specs=[pl.BlockSpec((tm, D), lambda i: (i, 0)), pl.BlockSpec((1, D), lambda i: (0, 0)),
                  pl.BlockSpec((tm, 1), lambda i: (i, 0)), pl.BlockSpec((1, LANES), lambda i: (0, 0)),
                  w_spec],
        out_specs=[pl.BlockSpec((tm, D), lambda i: (i, 0))] + [tab_spec] * 3 + seg_specs,
        out_shape=[jax.ShapeDtypeStruct((T, D), BF16)] + [tab_shape] * 3 + seg_shapes,
        compiler_params=_params(1), name="rmsnorm0")(x, g, pos, invf_lanes, w_in32)


PROJ_TM = 1024
PROJ_SUB_M = 128
PROJ_SUB = 256


def _rope(r, c, s1, s2):
    return (r * c + pltpu.roll(r, LANES - ROPE_HALF, 1) * s1
            + pltpu.roll(r, ROPE_HALF, 1) * s2)


def _sub_tiles(xn_ref, w_ref, b_ref):
    for mi in range(xn_ref.shape[0] // PROJ_SUB_M):
        rows = slice(mi * PROJ_SUB_M, (mi + 1) * PROJ_SUB_M)
        xn = xn_ref[rows, :]
        for si in range(w_ref.shape[1] // PROJ_SUB):
            sl = slice(si * PROJ_SUB, (si + 1) * PROJ_SUB)
            yield rows, si, jnp.dot(xn, w_ref[:, sl], preferred_element_type=F32) + b_ref[:, sl]


def _cast_blocks(rest):
    n = len(rest) // 2
    for src, dst in zip(rest[:n], rest[n + 1:]):
        dst[...] = src[...].astype(BF16)
    return rest[n]


def _gates_kernel(xn_ref, w_ref, b_ref, *rest):
    o_ref = _cast_blocks(rest)
    for rows, si, r in _sub_tiles(xn_ref, w_ref, b_ref):
        o_ref[rows, si * PROJ_SUB:(si + 1) * PROJ_SUB] = (
            1.0 / (1.0 + jnp.exp2(r * -LOG2E))).astype(o_ref.dtype)


def _gelu_kernel(xn_ref, w_ref, b_ref, *rest):
    o_ref = _cast_blocks(rest)
    for rows, si, r in _sub_tiles(xn_ref, w_ref, b_ref):
        o_ref[rows, si * PROJ_SUB:(si + 1) * PROJ_SUB] = (
            0.5 * r * (1.0 + lax.erf(r * (2.0 ** -0.5)))).astype(o_ref.dtype)


def _qkv_kernel(xn_ref, w_ref, b_ref, c_ref, s1_ref, s2_ref, q_ref, kv_ref):
    low = _low_half_lanes()
    for rows, si, r in _sub_tiles(xn_ref, w_ref, b_ref):
        c, s1, s2 = c_ref[rows, :], s1_ref[rows, :], s2_ref[rows, :]
        for gi in range(PROJ_SUB // LANES):
            x = r[:, gi * LANES:(gi + 1) * LANES]
            col = si * PROJ_SUB + gi * LANES
            if col < ATTN_WIDTH:
                xq = _rope(x, c, s1, s2) * (HEAD_DIM ** -0.5 * LOG2E)
                q_ref[rows, 2 * col:2 * col + LANES] = jnp.where(low, xq, 0.0).astype(q_ref.dtype)
                q_ref[rows, 2 * col + LANES:2 * col + 2 * LANES] = jnp.where(low, 0.0, xq).astype(q_ref.dtype)
                continue
            if col < ATTN_WIDTH + KV_WIDTH:
                x = _rope(x, c, s1, s2)
            swapped = pltpu.roll(x, HEAD_DIM, 1)
            lo = 2 * (col - ATTN_WIDTH)
            kv_ref[rows, lo:lo + LANES] = jnp.where(low, x, swapped).astype(kv_ref.dtype)
            kv_ref[rows, lo + LANES:lo + 2 * LANES] = jnp.where(low, swapped, x).astype(kv_ref.dtype)


def _qkv_proj(xn, w, b, tabs):
    T, tm = xn.shape[0], PROJ_TM
    width = w.shape[-1]
    tab_spec = pl.BlockSpec((tm, LANES), lambda i: (i, 0))
    return pl.pallas_call(
        _qkv_kernel, grid=(T // tm,),
        in_specs=[pl.BlockSpec((tm, D_MODEL), lambda i: (i, 0)),
                  pl.BlockSpec((None, D_MODEL, width), lambda i: (0, 0, 0)),
                  pl.BlockSpec((None, 1, width), lambda i: (0, 0, 0))] + [tab_spec] * 3,
        out_specs=[pl.BlockSpec((tm, 2 * ATTN_WIDTH), lambda i: (i, 0)),
                   pl.BlockSpec((tm, 2 * KVDUP_WIDTH), lambda i: (i, 0))],
        out_shape=[jax.ShapeDtypeStruct((T, 2 * ATTN_WIDTH), BF16),
                   jax.ShapeDtypeStruct((T, 2 * KVDUP_WIDTH), BF16)],
        compiler_params=_params(1), name="proj_qkv")(xn, w, b, *tabs)


def _proj(body, name, xn, w, b, tn, cast):
    T, tm = xn.shape[0], PROJ_TM
    width = w.shape[-1]
    n_i, n_j = T // tm, width // tn
    in_specs = [pl.BlockSpec((tm, D_MODEL), lambda i, j: (i, 0)),
                pl.BlockSpec((None, D_MODEL, tn), lambda i, j: (0, 0, j)),
                pl.BlockSpec((None, 1, tn), lambda i, j: (0, 0, j))]
    out_specs = [pl.BlockSpec((tm, tn), lambda i, j: (i, j))]
    out_shape = [jax.ShapeDtypeStruct((T, width), BF16)]
    args = [xn, w, b]
    arrays, layer = cast
    for a in arrays:
        blk = (None, a.shape[1] // n_i, a.shape[2] // n_j)
        in_specs.append(pl.BlockSpec(blk, lambda i, j: (layer, i, j)))
        out_specs.append(pl.BlockSpec(blk, lambda i, j: (0, i, j)))
        out_shape.append(jax.ShapeDtypeStruct((1,) + a.shape[1:], BF16))
        args.append(a)
    return pl.pallas_call(
        body, grid=(n_i, n_j), in_specs=in_specs, out_specs=out_specs, out_shape=out_shape,
        compiler_params=_params(2), name=name)(*args)


def _attention_rows(seq_start, sinks_ref, q_ref, kc_ref, vc_ref, kp_ref, vp_ref, o_ref):
    qi = lax.broadcasted_iota(jnp.int32, (WINDOW, 2 * WINDOW), 0)
    kj = lax.broadcasted_iota(jnp.int32, (WINDOW, 2 * WINDOW), 1)
    rel = qi + WINDOW - kj
    band_ok = (rel >= 0) & (rel < WINDOW)
    low = _low_half_lanes()
    ones_low = jnp.broadcast_to(jnp.where(low, 1.0, 0.0).astype(BF16), (2 * WINDOW, LANES))
    ones_high = jnp.broadcast_to(jnp.where(low, 0.0, 1.0).astype(BF16), (2 * WINDOW, LANES))
    for c in range(q_ref.shape[0] // WINDOW):
        rows = slice(c * WINDOW, (c + 1) * WINDOW)
        if c == 0:
            first_key = jnp.where(seq_start, WINDOW, 0)
            mask = band_ok & (kj >= first_key)
            k_prev, v_prev = kp_ref[...], vp_ref[...]
        else:
            mask = band_ok
            prev_rows = slice((c - 1) * WINDOW, c * WINDOW)
            k_prev, v_prev = kc_ref[prev_rows, :], vc_ref[prev_rows, :]
        k_band = jnp.concatenate([k_prev, kc_ref[rows, :]], axis=0)
        v_band = jnp.concatenate([v_prev, vc_ref[rows, :]], axis=0)
        for g in range(N_KV_HEADS):
            lanes_g = slice(g * LANES, (g + 1) * LANES)
            kg, vg = k_band[:, lanes_g], v_band[:, lanes_g]
            zero = jnp.zeros_like(vg)
            rhs_pv = jnp.concatenate(
                [jnp.concatenate([jnp.where(low, vg, zero), ones_low], axis=1),
                 jnp.concatenate([jnp.where(low, zero, vg), ones_high], axis=1)], axis=0)
            lhs = jnp.concatenate(
                [q_ref[rows, (Q_PER_KV * g + i) * LANES:(Q_PER_KV * g + i + 1) * LANES]
                 for i in range(Q_PER_KV)], axis=0)
            s = lax.dot_general(lhs, kg, (((1,), (1,)), ((), ())), preferred_element_type=F32)
            probs, sink_terms = [], []
            for i in range(Q_PER_KV):
                sink = sinks_ref[g * Q_PER_KV + i] * LOG2E
                si = jnp.where(mask, s[i * WINDOW:(i + 1) * WINDOW, :], NEG)
                m = jnp.maximum(jnp.max(si, axis=-1, keepdims=True), sink)
                probs.append(jnp.exp2(si - m).astype(BF16))
                sink_terms.append(jnp.exp2(sink - m))
            for pair in range(2):
                p2 = jnp.concatenate(probs[2 * pair:2 * pair + 2], axis=1)
                r = jnp.dot(p2, rhs_pv, preferred_element_type=F32)
                den = r[:, LANES:] + jnp.where(low, sink_terms[2 * pair], sink_terms[2 * pair + 1])
                out_lanes = slice((2 * g + pair) * LANES, (2 * g + pair + 1) * LANES)
                o_ref[rows, out_lanes] = (r[:, :LANES] / den).astype(o_ref.dtype)


def _sgu_norm(v_ref, lng_ref, lnb_ref):
    v = v_ref[...].astype(F32)
    mu = jnp.mean(v, axis=-1, keepdims=True)
    d = v - mu
    var = jnp.mean(d * d, axis=-1, keepdims=True)
    return (d * lax.rsqrt(var + EPS) * lng_ref[...] + lnb_ref[...]).astype(BF16)


def _sgu_group(g, vn, u_ref, w_ref, b_ref, o_ref):
    ti = lax.broadcasted_iota(jnp.int32, (CHUNK, CHUNK), 0)
    si = lax.broadcasted_iota(jnp.int32, (CHUNK, CHUNK), 1)
    cols = slice(g * SGU_GROUP_DIM, (g + 1) * SGU_GROUP_DIM)
    w = jnp.where(si <= ti, w_ref[g], jnp.zeros_like(w_ref[g]))
    bias = b_ref[g]
    for c in range(u_ref.shape[0] // CHUNK):
        rows = slice(c * CHUNK, (c + 1) * CHUNK)
        sv = jnp.dot(w, vn[rows, cols], preferred_element_type=F32) + bias
        o_ref[rows, cols] = (u_ref[rows, cols].astype(F32) * sv).astype(o_ref.dtype)


MIX_TM = 256
MRG_TN = 1024


def _mixer_kernel(blocks_per_seq, sinks_ref, q_ref, kc_ref, vc_ref, kp_ref, vp_ref, u_ref, v_ref,
                  lng_ref, lnb_ref, sw_ref, sb_ref, g_ref, h_ref, wa_ref, ws_ref, wo_ref, g2_ref, *rest):
    if len(rest) == 6:
        h1_ref, hn_ref, ya_ref, ys_ref, part_ref, merged_ref = rest
    else:
        next_in_ref, h1_ref, hn_ref, *seg_refs, ya_ref, ys_ref, part_ref, merged_ref = rest
        _cast_in_segments(next_in_ref, seg_refs)
    seq_start = (pl.program_id(0) * (MIX_TM // WINDOW)) % blocks_per_seq == 0
    _attention_rows(seq_start, sinks_ref, q_ref, kc_ref, vc_ref, kp_ref, vp_ref, ya_ref)
    ya = ya_ref[...]
    vn = _sgu_norm(v_ref, lng_ref, lnb_ref)
    groups_per_tile = SGU_GROUPS // (D_MODEL // MRG_TN)
    for j in range(D_MODEL // MRG_TN):
        cols = slice(j * MRG_TN, (j + 1) * MRG_TN)
        part_ref[:, cols] = g_ref[:, cols].astype(F32) * jnp.dot(
            ya, wa_ref[:, cols], preferred_element_type=F32)
        for g in range(j * groups_per_tile, (j + 1) * groups_per_tile):
            _sgu_group(g, vn, u_ref, sw_ref, sb_ref, ys_ref)
    ys = ys_ref[...]
    for j in range(D_MODEL // MRG_TN):
        cols = slice(j * MRG_TN, (j + 1) * MRG_TN)
        gs_cols = slice(D_MODEL + j * MRG_TN, D_MODEL + (j + 1) * MRG_TN)
        s = jnp.dot(ys, ws_ref[:, cols], preferred_element_type=F32)
        merged_ref[:, cols] = (g_ref[:, gs_cols].astype(F32) * s + part_ref[:, cols]).astype(BF16)
    merged = merged_ref[...]
    for j in range(D_MODEL // MRG_TN):
        cols = slice(j * MRG_TN, (j + 1) * MRG_TN)
        h1_ref[:, cols] = h_ref[:, cols] + jnp.dot(merged, wo_ref[:, cols], preferred_element_type=F32)
    hn_ref[...] = _rmsnorm_rows(h1_ref[...], g2_ref[...]).astype(hn_ref.dtype)


def _mixer(q, kv, z, gates, h, sinks, ln_g, ln_b, sgu_w, sgu_b, w_ab, w_sb, w_out, g2, seq_len,
           cast_in=None):
    T, tm = h.shape[0], MIX_TM
    n_rows = T // tm
    row = lambda width, col=0: pl.BlockSpec((tm, width), lambda i: (i, col))
    prev = lambda col: pl.BlockSpec((WINDOW, KVDUP_WIDTH),
                                    lambda i: (jnp.maximum(i * (tm // WINDOW) - 1, 0), col))
    const = lambda shape: pl.BlockSpec(shape, lambda i: (0,) * len(shape))
    resident = lambda shape: pl.BlockSpec((None,) + shape, lambda i: (0, 0, 0),
                                          pipeline_mode=pl.Buffered(1))
    in_specs = [pl.BlockSpec(memory_space=pltpu.SMEM),
                row(2 * ATTN_WIDTH), row(KVDUP_WIDTH, 0), row(KVDUP_WIDTH, 1), prev(0), prev(1),
                row(SGU_WIDTH, 0), row(SGU_WIDTH, 1), const((1, SGU_WIDTH)), const((1, SGU_WIDTH)),
                const((SGU_GROUPS, CHUNK, CHUNK)), const((SGU_GROUPS, CHUNK, 1)),
                row(2 * D_MODEL), row(D_MODEL),
                resident((ATTN_WIDTH, D_MODEL)), resident((SGU_WIDTH, D_MODEL)),
                resident((D_MODEL, D_MODEL)), resident((1, D_MODEL))]
    out_specs = [row(D_MODEL), row(D_MODEL)]
    out_shape = [jax.ShapeDtypeStruct((T, D_MODEL), F32), jax.ShapeDtypeStruct((T, D_MODEL), BF16)]
    args = [sinks, q, kv, kv, kv, kv, z, z, ln_g, ln_b, sgu_w, sgu_b, gates, h, w_ab, w_sb, w_out, g2]
    if cast_in is not None:
        w_spec, seg_specs, seg_shapes = _in_segment_specs(*cast_in, n_rows)
        in_specs.append(w_spec)
        args.append(cast_in[0])
        out_specs += seg_specs
        out_shape += seg_shapes
    return pl.pallas_call(
        functools.partial(_mixer_kernel, seq_len // WINDOW), grid=(n_rows,),
        in_specs=in_specs, out_specs=out_specs, out_shape=out_shape,
        scratch_shapes=[pltpu.VMEM((tm, ATTN_WIDTH), BF16), pltpu.VMEM((tm, SGU_WIDTH), BF16),
                        pltpu.VMEM((tm, D_MODEL), F32), pltpu.VMEM((tm, D_MODEL), BF16)],
        compiler_params=_params(1), name="token_mixer")(*args)


FFN_TM = 1024
FFN_TF = 512
FFN_SUB = 256
FFN_DOWN_TN = 512


def _ffn_kernel(last, hn_hbm, h1_hbm, wg_ref, wu_ref, wd_ref, gn_ref, *rest):
    if last:
        out_hbm, acc, act, load_sem, store_sem = rest
    else:
        h2_hbm, xn_hbm, acc, act, load_sem, store_sem = rest
    i, f = pl.program_id(0), pl.program_id(1)
    n_rows, nf = pl.num_programs(0), pl.num_programs(1)
    slot = i % 2

    def loads(tile, s):
        rows = pl.ds(pl.multiple_of(tile * FFN_TM, FFN_TM), FFN_TM)
        return (pltpu.make_async_copy(h1_hbm.at[rows, :], acc.at[s], load_sem.at[s, 0]),
                pltpu.make_async_copy(hn_hbm.at[rows, :], act.at[s], load_sem.at[s, 1]))

    def stores(tile, s):
        rows = pl.ds(pl.multiple_of(tile * FFN_TM, FFN_TM), FFN_TM)
        if last:
            return (pltpu.make_async_copy(acc.at[s], out_hbm.at[rows, :], store_sem.at[s, 0]),)
        return (pltpu.make_async_copy(acc.at[s], h2_hbm.at[rows, :], store_sem.at[s, 0]),
                pltpu.make_async_copy(act.at[s], xn_hbm.at[rows, :], store_sem.at[s, 1]))

    @pl.when((i == 0) & (f == 0))
    def _():
        for c in loads(0, 0):
            c.start()

    @pl.when(f == 0)
    def _():
        for c in loads(i, slot):
            c.wait()

    @pl.when((f == 1) & (i >= 1))
    def _():
        for c in stores(i - 1, 1 - slot):
            c.wait()

    @pl.when((f == 1) & (i + 1 < n_rows))
    def _():
        for c in loads(i + 1, 1 - slot):
            c.start()

    acc_s, act_s = acc.at[slot], act.at[slot]
    hn = act_s[...]
    acts = []
    for si in range(FFN_TF // FFN_SUB):
        sl = slice(si * FFN_SUB, (si + 1) * FFN_SUB)
        gate = jnp.dot(hn, wg_ref[:, sl], preferred_element_type=F32)
        up = jnp.dot(hn, wu_ref[:, sl], preferred_element_type=F32)
        acts.append((gate * (1.0 / (1.0 + jnp.exp(-gate))) * up).astype(BF16))
    a = jnp.concatenate(acts, axis=1)
    for ci in range(D_MODEL // FFN_DOWN_TN):
        cols = slice(ci * FFN_DOWN_TN, (ci + 1) * FFN_DOWN_TN)
        acc_s[:, cols] += jnp.dot(a, wd_ref[:, cols], preferred_element_type=F32)

    @pl.when(f == nf - 1)
    def _():
        y = _rmsnorm_rows(acc_s[...], gn_ref[...])
        if last:
            acc_s[...] = y
        else:
            act_s[...] = y.astype(BF16)
        for c in stores(i, slot):
            c.start()

    @pl.when((f == nf - 1) & (i == n_rows - 1))
    def _():
        for c in stores(i, slot):
            c.wait()


def _ffn(hn, h1, w_gu, w_down, g_next, last):
    T = h1.shape[0]
    nf = D_FF // FFN_TF
    hbm = pl.BlockSpec(memory_space=pl.ANY)
    rows_f32 = jax.ShapeDtypeStruct((T, D_MODEL), F32)
    return pl.pallas_call(
        functools.partial(_ffn_kernel, last), grid=(T // FFN_TM, nf),
        in_specs=[hbm, hbm,
                  pl.BlockSpec((None, D_MODEL, FFN_TF), lambda i, f: (0, 0, f)),
                  pl.BlockSpec((None, D_MODEL, FFN_TF), lambda i, f: (0, 0, nf + f)),
                  pl.BlockSpec((None, FFN_TF, D_MODEL), lambda i, f: (0, f, 0)),
                  pl.BlockSpec((1, D_MODEL), lambda i, f: (0, 0))],
        out_specs=hbm if last else [hbm, hbm],
        out_shape=rows_f32 if last else [rows_f32, jax.ShapeDtypeStruct((T, D_MODEL), BF16)],
        scratch_shapes=[pltpu.VMEM((2, FFN_TM, D_MODEL), F32), pltpu.VMEM((2, FFN_TM, D_MODEL), BF16),
                        pltpu.SemaphoreType.DMA((2, 2)), pltpu.SemaphoreType.DMA((2, 2))],
        compiler_params=_params(2), name="swiglu_ffn")(hn, h1, w_gu, w_gu, w_down, g_next)


def kernel(x, positions, norm1_g, w_in, b_in, sinks, sgu_ln_g, sgu_ln_b, sgu_w, sgu_b,
           w_attn_branch, w_sgu_branch, w_out, norm2_g, w_gate_up, w_down, final_g):
    B, S, D = x.shape
    T = B * S
    depth = w_in.shape[0]
    assert D == D_MODEL and S % MIX_TM == 0 and T % FFN_TM == 0 and w_in.shape[-1] == IN_WIDTH

    inv_freq = ROPE_THETA ** (-jnp.arange(0, ROPE_DIM, 2, dtype=F32) / ROPE_DIM)
    lane = jnp.arange(LANES) % HEAD_DIM
    invf_lanes = jnp.where(lane < ROPE_DIM, inv_freq[lane % ROPE_HALF], 0.0)[None, :]

    sgu_w_b, sgu_b_c = sgu_w.astype(BF16), sgu_b[..., None]

    h = x.reshape(T, D)
    xn, *tabs, w_qkv, w_z, w_g = _rmsnorm(h, norm1_g[0][None, :], positions.reshape(T, 1), invf_lanes, w_in)
    for l in range(depth):
        b_qkv, b_z, b_g = (b_in[l, lo:hi][None, None, :] for lo, hi in IN_SEGMENTS)
        gates, w_gu, w_dn = _proj(_gates_kernel, "proj_gates", xn, w_g, b_g, tn=2048,
                                  cast=((w_gate_up, w_down), l))
        z, w_ab, w_sb, w_o = _proj(_gelu_kernel, "proj_gelu", xn, w_z, b_z, tn=2048,
                                   cast=((w_attn_branch, w_sgu_branch, w_out), l))
        q, kv = _qkv_proj(xn, w_qkv, b_qkv, tabs)
        mix = functools.partial(_mixer, q, kv, z, gates, h, sinks[l], sgu_ln_g[l][None, :],
                                sgu_ln_b[l][None, :], sgu_w_b[l], sgu_b_c[l], w_ab, w_sb, w_o,
                                norm2_g[l][None, None, :], S)
        if l == depth - 1:
            h1, hn = mix()
            out = _ffn(hn, h1, w_gu, w_dn, final_g[None, :], last=True)
        else:
            h1, hn, w_qkv, w_z, w_g = mix(cast_in=(w_in, l + 1))
            h, xn = _ffn(hn, h1, w_gu, w_dn, norm1_g[l + 1][None, :], last=False)
    return out.reshape(B, S, D)
```

```python
import functools

import jax
import jax.numpy as jnp
from jax import lax
from jax.experimental import pallas as pl
from jax.experimental.pallas import tpu as pltpu

D_MODEL = 2048
N_Q_HEADS = 16
N_KV_HEADS = 4
HEAD_DIM = 64
Q_PER_KV = N_Q_HEADS // N_KV_HEADS
ATTN_WIDTH = N_Q_HEADS * HEAD_DIM
KV_WIDTH = N_KV_HEADS * HEAD_DIM
WINDOW = 128
ROPE_THETA = 500000.0
ROPE_DIM = HEAD_DIM // 4
ROPE_HALF = ROPE_DIM // 2
SGU_WIDTH = D_MODEL // 2
SGU_GROUPS = 8
SGU_GROUP_DIM = SGU_WIDTH // SGU_GROUPS
CHUNK = 128
D_FF = 5632
EPS = 1e-5
NEG = -1e30
LOG2E = 1.4426950408889634

OFF_Q = 0
OFF_K = OFF_Q + ATTN_WIDTH
OFF_Z = OFF_K + 2 * KV_WIDTH
OFF_G = OFF_Z + 2 * SGU_WIDTH
IN_WIDTH = OFF_G + 2 * D_MODEL
IN_SEGMENTS = ((OFF_Q, OFF_Z), (OFF_Z, OFF_G), (OFF_G, IN_WIDTH))
KVDUP_WIDTH = 2 * KV_WIDTH

LANES = 128
VMEM_LIMIT = 52 * 1024 * 1024

BF16 = jnp.bfloat16
F32 = jnp.float32


def _params(n_axes):
    return pltpu.CompilerParams(dimension_semantics=("arbitrary",) * n_axes,
                                vmem_limit_bytes=VMEM_LIMIT)


def _rmsnorm_rows(x, g):
    return x * lax.rsqrt(jnp.mean(x * x, axis=-1, keepdims=True) + EPS) * g


def _low_half_lanes(shape=(1, LANES)):
    return lax.broadcasted_iota(jnp.int32, shape, len(shape) - 1) % LANES < HEAD_DIM


def _rope_tables(pos_ref, invf_ref, c_ref, s1_ref, s2_ref):
    lane = lax.broadcasted_iota(jnp.int32, c_ref.shape, 1) % HEAD_DIM
    ang = pos_ref[...].astype(F32) * invf_ref[...]
    cos, sin = jnp.cos(ang), jnp.sin(ang)
    first, second = lane < ROPE_HALF, (lane >= ROPE_HALF) & (lane < ROPE_DIM)
    c_ref[...] = jnp.where(lane < ROPE_DIM, cos, 1.0)
    s1_ref[...] = jnp.where(first, -sin, 0.0)
    s2_ref[...] = jnp.where(second, sin, 0.0)


def _cast_in_segments(w32_ref, seg_refs):
    for (lo, hi), seg_ref in zip(IN_SEGMENTS, seg_refs):
        seg_ref[...] = w32_ref[:, lo:hi].astype(BF16)


def _in_segment_specs(w_in32, layer, n_steps):
    rows = D_MODEL // n_steps
    in_spec = pl.BlockSpec((None, rows, IN_WIDTH), lambda i: (layer, i, 0))
    out_specs = [pl.BlockSpec((None, rows, hi - lo), lambda i: (0, i, 0)) for lo, hi in IN_SEGMENTS]
    out_shape = [jax.ShapeDtypeStruct((1, D_MODEL, hi - lo), BF16) for lo, hi in IN_SEGMENTS]
    return in_spec, out_specs, out_shape


def _norm_kernel(x_ref, g_ref, pos_ref, invf_ref, w32_ref, o_ref, c_ref, s1_ref, s2_ref, *seg_refs):
    _rope_tables(pos_ref, invf_ref, c_ref, s1_ref, s2_ref)
    _cast_in_segments(w32_ref, seg_refs)
    o_ref[...] = _rmsnorm_rows(x_ref[...], g_ref[...]).astype(o_ref.dtype)


def _rmsnorm(x, g, pos, invf_lanes, w_in32, tm=512):
    T, D = x.shape
    w_spec, seg_specs, seg_shapes = _in_segment_specs(w_in32, 0, T // tm)
    tab_spec = pl.BlockSpec((tm, LANES), lambda i: (i, 0))
    tab_shape = jax.ShapeDtypeStruct((T, LANES), F32)
    return pl.pallas_call(
        _norm_kernel, grid=(T // tm,),
        in_specs=[pl.BlockSpec((tm, D), lambda i: (i, 0)), pl.BlockSpec((1, D), lambda i: (0, 0)),
                  pl.BlockSpec((tm, 1), lambda i: (i, 0)), pl.BlockSpec((1, LANES), lambda i: (0, 0)),
                  w_spec],
        out_specs=[pl.BlockSpec((tm, D), lambda i: (i, 0))] + [tab_spec] * 3 + seg_specs,
        out_shape=[jax.ShapeDtypeStruct((T, D), BF16)] + [tab_shape] * 3 + seg_shapes,
        compiler_params=_params(1), name="rmsnorm0")(x, g, pos, invf_lanes, w_in32)


PROJ_TM = 1024
PROJ_SUB_M = 128
PROJ_SUB = 256


def _rope(r, c, s1, s2):
    return (r * c + pltpu.roll(r, LANES - ROPE_HALF, 1) * s1
            + pltpu.roll(r, ROPE_HALF, 1) * s2)


def _sub_tiles(xn_ref, w_ref, b_ref):
    for mi in range(xn_ref.shape[0] // PROJ_SUB_M):
        rows = slice(mi * PROJ_SUB_M, (mi + 1) * PROJ_SUB_M)
        xn = xn_ref[rows, :]
        for si in range(w_ref.shape[1] // PROJ_SUB):
            sl = slice(si * PROJ_SUB, (si + 1) * PROJ_SUB)
            yield rows, si, jnp.dot(xn, w_ref[:, sl], preferred_element_type=F32) + b_ref[:, sl]


def _cast_blocks(rest):
    n = len(rest) // 2
    for src, dst in zip(rest[:n], rest[n + 1:]):
        dst[...] = src[...].astype(BF16)
    return rest[n]


def _gates_kernel(xn_ref, w_ref, b_ref, *rest):
    o_ref = _cast_blocks(rest)
    for rows, si, r in _sub_tiles(xn_ref, w_ref, b_ref):
        o_ref[rows, si * PROJ_SUB:(si + 1) * PROJ_SUB] = (
            1.0 / (1.0 + jnp.exp2(r * -LOG2E))).astype(o_ref.dtype)


def _gelu_kernel(xn_ref, w_ref, b_ref, *rest):
    o_ref = _cast_blocks(rest)
    for rows, si, r in _sub_tiles(xn_ref, w_ref, b_ref):
        o_ref[rows, si * PROJ_SUB:(si + 1) * PROJ_SUB] = (
            0.5 * r * (1.0 + lax.erf(r * (2.0 ** -0.5)))).astype(o_ref.dtype)


def _qkv_kernel(xn_ref, w_ref, b_ref, c_ref, s1_ref, s2_ref, q_ref, kv_ref):
    low = _low_half_lanes()
    for rows, si, r in _sub_tiles(xn_ref, w_ref, b_ref):
        c, s1, s2 = c_ref[rows, :], s1_ref[rows, :], s2_ref[rows, :]
        for gi in range(PROJ_SUB // LANES):
            x = r[:, gi * LANES:(gi + 1) * LANES]
            col = si * PROJ_SUB + gi * LANES
            if col < ATTN_WIDTH:
                xq = _rope(x, c, s1, s2) * (HEAD_DIM ** -0.5 * LOG2E)
                q_ref[rows, 2 * col:2 * col + LANES] = jnp.where(low, xq, 0.0).astype(q_ref.dtype)
                q_ref[rows, 2 * col + LANES:2 * col + 2 * LANES] = jnp.where(low, 0.0, xq).astype(q_ref.dtype)
                continue
            if col < ATTN_WIDTH + KV_WIDTH:
                x = _rope(x, c, s1, s2)
            swapped = pltpu.roll(x, HEAD_DIM, 1)
            lo = 2 * (col - ATTN_WIDTH)
            kv_ref[rows, lo:lo + LANES] = jnp.where(low, x, swapped).astype(kv_ref.dtype)
            kv_ref[rows, lo + LANES:lo + 2 * LANES] = jnp.where(low, swapped, x).astype(kv_ref.dtype)


def _qkv_proj(xn, w, b, tabs):
    T, tm = xn.shape[0], PROJ_TM
    width = w.shape[-1]
    tab_spec = pl.BlockSpec((tm, LANES), lambda i: (i, 0))
    return pl.pallas_call(
        _qkv_kernel, grid=(T // tm,),
        in_specs=[pl.BlockSpec((tm, D_MODEL), lambda i: (i, 0)),
                  pl.BlockSpec((None, D_MODEL, width), lambda i: (0, 0, 0)),
                  pl.BlockSpec((None, 1, width), lambda i: (0, 0, 0))] + [tab_spec] * 3,
        out_specs=[pl.BlockSpec((tm, 2 * ATTN_WIDTH), lambda i: (i, 0)),
                   pl.BlockSpec((tm, 2 * KVDUP_WIDTH), lambda i: (i, 0))],
        out_shape=[jax.ShapeDtypeStruct((T, 2 * ATTN_WIDTH), BF16),
                   jax.ShapeDtypeStruct((T, 2 * KVDUP_WIDTH), BF16)],
        compiler_params=_params(1), name="proj_qkv")(xn, w, b, *tabs)


def _proj(body, name, xn, w, b, tn, cast):
    T, tm = xn.shape[0], PROJ_TM
    width = w.shape[-1]
    n_i, n_j = T // tm, width // tn
    in_specs = [pl.BlockSpec((tm, D_MODEL), lambda i, j: (i, 0)),
                pl.BlockSpec((None, D_MODEL, tn), lambda i, j: (0, 0, j)),
                pl.BlockSpec((None, 1, tn), lambda i, j: (0, 0, j))]
    out_specs = [pl.BlockSpec((tm, tn), lambda i, j: (i, j))]
    out_shape = [jax.ShapeDtypeStruct((T, width), BF16)]
    args = [xn, w, b]
    arrays, layer = cast
    for a in arrays:
        blk = (None, a.shape[1] // n_i, a.shape[2] // n_j)
        in_specs.append(pl.BlockSpec(blk, lambda i, j: (layer, i, j)))
        out_specs.append(pl.BlockSpec(blk, lambda i, j: (0, i, j)))
        out_shape.append(jax.ShapeDtypeStruct((1,) + a.shape[1:], BF16))
        args.append(a)
    return pl.pallas_call(
        body, grid=(n_i, n_j), in_specs=in_specs, out_specs=out_specs, out_shape=out_shape,
        compiler_params=_params(2), name=name)(*args)


def _attention_rows(seq_start, sinks_ref, q_ref, kc_ref, vc_ref, kp_ref, vp_ref, o_ref):
    qi = lax.broadcasted_iota(jnp.int32, (WINDOW, 2 * WINDOW), 0)
    kj = lax.broadcasted_iota(jnp.int32, (WINDOW, 2 * WINDOW), 1)
    rel = qi + WINDOW - kj
    band_ok = (rel >= 0) & (rel < WINDOW)
    low = _low_half_lanes()
    ones_low = jnp.broadcast_to(jnp.where(low, 1.0, 0.0).astype(BF16), (2 * WINDOW, LANES))
    ones_high = jnp.broadcast_to(jnp.where(low, 0.0, 1.0).astype(BF16), (2 * WINDOW, LANES))
    for c in range(q_ref.shape[0] // WINDOW):
        rows = slice(c * WINDOW, (c + 1) * WINDOW)
        if c == 0:
            first_key = jnp.where(seq_start, WINDOW, 0)
            mask = band_ok & (kj >= first_key)
            k_prev, v_prev = kp_ref[...], vp_ref[...]
        else:
            mask = band_ok
            prev_rows = slice((c - 1) * WINDOW, c * WINDOW)
            k_prev, v_prev = kc_ref[prev_rows, :], vc_ref[prev_rows, :]
        k_band = jnp.concatenate([k_prev, kc_ref[rows, :]], axis=0)
        v_band = jnp.concatenate([v_prev, vc_ref[rows, :]], axis=0)
        for g in range(N_KV_HEADS):
            lanes_g = slice(g * LANES, (g + 1) * LANES)
            kg, vg = k_band[:, lanes_g], v_band[:, lanes_g]
            zero = jnp.zeros_like(vg)
            rhs_pv = jnp.concatenate(
                [jnp.concatenate([jnp.where(low, vg, zero), ones_low], axis=1),
                 jnp.concatenate([jnp.where(low, zero, vg), ones_high], axis=1)], axis=0)
            lhs = jnp.concatenate(
                [q_ref[rows, (Q_PER_KV * g + i) * LANES:(Q_PER_KV * g + i + 1) * LANES]
                 for i in range(Q_PER_KV)], axis=0)
            s = lax.dot_general(lhs, kg, (((1,), (1,)), ((), ())), preferred_element_type=F32)
            probs, sink_terms = [], []
            for i in range(Q_PER_KV):
                sink = sinks_ref[g * Q_PER_KV + i] * LOG2E
                si = jnp.where(mask, s[i * WINDOW:(i + 1) * WINDOW, :], NEG)
                m = jnp.maximum(jnp.max(si, axis=-1, keepdims=True), sink)
                probs.append(jnp.exp2(si - m).astype(BF16))
                sink_terms.append(jnp.exp2(sink - m))
            for pair in range(2):
                p2 = jnp.concatenate(probs[2 * pair:2 * pair + 2], axis=1)
                r = jnp.dot(p2, rhs_pv, preferred_element_type=F32)
                den = r[:, LANES:] + jnp.where(low, sink_terms[2 * pair], sink_terms[2 * pair + 1])
                out_lanes = slice((2 * g + pair) * LANES, (2 * g + pair + 1) * LANES)
                o_ref[rows, out_lanes] = (r[:, :LANES] / den).astype(o_ref.dtype)


def _sgu_norm(v_ref, lng_ref, lnb_ref):
    v = v_ref[...].astype(F32)
    mu = jnp.mean(v, axis=-1, keepdims=True)
    d = v - mu
    var = jnp.mean(d * d, axis=-1, keepdims=True)
    return (d * lax.rsqrt(var + EPS) * lng_ref[...] + lnb_ref[...]).astype(BF16)


def _sgu_group(g, vn, u_ref, w_ref, b_ref, o_ref):
    ti = lax.broadcasted_iota(jnp.int32, (CHUNK, CHUNK), 0)
    si = lax.broadcasted_iota(jnp.int32, (CHUNK, CHUNK), 1)
    cols = slice(g * SGU_GROUP_DIM, (g + 1) * SGU_GROUP_DIM)
    w = jnp.where(si <= ti, w_ref[g], jnp.zeros_like(w_ref[g]))
    bias = b_ref[g]
    for c in range(u_ref.shape[0] // CHUNK):
        rows = slice(c * CHUNK, (c + 1) * CHUNK)
        sv = jnp.dot(w, vn[rows, cols], preferred_element_type=F32) + bias
        o_ref[rows, cols] = (u_ref[rows, cols].astype(F32) * sv).astype(o_ref.dtype)


MIX_TM = 256
MRG_TN = 1024


def _mixer_kernel(blocks_per_seq, sinks_ref, q_ref, kc_ref, vc_ref, kp_ref, vp_ref, u_ref, v_ref,
                  lng_ref, lnb_ref, sw_ref, sb_ref, g_ref, h_ref, wa_ref, ws_ref, wo_ref, g2_ref, *rest):
    if len(rest) == 6:
        h1_ref, hn_ref, ya_ref, ys_ref, part_ref, merged_ref = rest
    else:
        next_in_ref, h1_ref, hn_ref, *seg_refs, ya_ref, ys_ref, part_ref, merged_ref = rest
        _cast_in_segments(next_in_ref, seg_refs)
    seq_start = (pl.program_id(0) * (MIX_TM // WINDOW)) % blocks_per_seq == 0
    _attention_rows(seq_start, sinks_ref, q_ref, kc_ref, vc_ref, kp_ref, vp_ref, ya_ref)
    ya = ya_ref[...]
    vn = _sgu_norm(v_ref, lng_ref, lnb_ref)
    groups_per_tile = SGU_GROUPS // (D_MODEL // MRG_TN)
    for j in range(D_MODEL // MRG_TN):
        cols = slice(j * MRG_TN, (j + 1) * MRG_TN)
        part_ref[:, cols] = g_ref[:, cols].astype(F32) * jnp.dot(
            ya, wa_ref[:, cols], preferred_element_type=F32)
        for g in range(j * groups_per_tile, (j + 1) * groups_per_tile):
            _sgu_group(g, vn, u_ref, sw_ref, sb_ref, ys_ref)
    ys = ys_ref[...]
    for j in range(D_MODEL // MRG_TN):
        cols = slice(j * MRG_TN, (j + 1) * MRG_TN)
        gs_cols = slice(D_MODEL + j * MRG_TN, D_MODEL + (j + 1) * MRG_TN)
        s = jnp.dot(ys, ws_ref[:, cols], preferred_element_type=F32)
        merged_ref[:, cols] = (g_ref[:, gs_cols].astype(F32) * s + part_ref[:, cols]).astype(BF16)
    merged = merged_ref[...]
    for j in range(D_MODEL // MRG_TN):
        cols = slice(j * MRG_TN, (j + 1) * MRG_TN)
        h1_ref[:, cols] = h_ref[:, cols] + jnp.dot(merged, wo_ref[:, cols], preferred_element_type=F32)
    hn_ref[...] = _rmsnorm_rows(h1_ref[...], g2_ref[...]).astype(hn_ref.dtype)


def _mixer(q, kv, z, gates, h, sinks, ln_g, ln_b, sgu_w, sgu_b, w_ab, w_sb, w_out, g2, seq_len,
           cast_in=None):
    T, tm = h.shape[0], MIX_TM
    n_rows = T // tm
    row = lambda width, col=0: pl.BlockSpec((tm, width), lambda i: (i, col))
    prev = lambda col: pl.BlockSpec((WINDOW, KVDUP_WIDTH),
                                    lambda i: (jnp.maximum(i * (tm // WINDOW) - 1, 0), col))
    const = lambda shape: pl.BlockSpec(shape, lambda i: (0,) * len(shape))
    resident = lambda shape: pl.BlockSpec((None,) + shape, lambda i: (0, 0, 0),
                                          pipeline_mode=pl.Buffered(1))
    in_specs = [pl.BlockSpec(memory_space=pltpu.SMEM),
                row(2 * ATTN_WIDTH), row(KVDUP_WIDTH, 0), row(KVDUP_WIDTH, 1), prev(0), prev(1),
                row(SGU_WIDTH, 0), row(SGU_WIDTH, 1), const((1, SGU_WIDTH)), const((1, SGU_WIDTH)),
                const((SGU_GROUPS, CHUNK, CHUNK)), const((SGU_GROUPS, CHUNK, 1)),
                row(2 * D_MODEL), row(D_MODEL),
                resident((ATTN_WIDTH, D_MODEL)), resident((SGU_WIDTH, D_MODEL)),
                resident((D_MODEL, D_MODEL)), resident((1, D_MODEL))]
    out_specs = [row(D_MODEL), row(D_MODEL)]
    out_shape = [jax.ShapeDtypeStruct((T, D_MODEL), F32), jax.ShapeDtypeStruct((T, D_MODEL), BF16)]
    args = [sinks, q, kv, kv, kv, kv, z, z, ln_g, ln_b, sgu_w, sgu_b, gates, h, w_ab, w_sb, w_out, g2]
    if cast_in is not None:
        w_spec, seg_specs, seg_shapes = _in_segment_specs(*cast_in, n_rows)
        in_specs.append(w_spec)
        args.append(cast_in[0])
        out_specs += seg_specs
        out_shape += seg_shapes
    return pl.pallas_call(
        functools.partial(_mixer_kernel, seq_len // WINDOW), grid=(n_rows,),
        in_specs=in_specs, out_specs=out_specs, out_shape=out_shape,
        scratch_shapes=[pltpu.VMEM((tm, ATTN_WIDTH), BF16), pltpu.VMEM((tm, SGU_WIDTH), BF16),
                        pltpu.VMEM((tm, D_MODEL), F32), pltpu.VMEM((tm, D_MODEL), BF16)],
        compiler_params=_params(1), name="token_mixer")(*args)


FFN_TM = 1024
FFN_TF = 512
FFN_SUB = 256
FFN_DOWN_TN = 512


def _ffn_kernel(last, hn_hbm, h1_hbm, wg_ref, wu_ref, wd_ref, gn_ref, *rest):
    if last:
        out_hbm, acc, act, load_sem, store_sem = rest
    else:
        h2_hbm, xn_hbm, acc, act, load_sem, store_sem = rest
    i, f = pl.program_id(0), pl.program_id(1)
    n_rows, nf = pl.num_programs(0), pl.num_programs(1)
    slot = i % 2

    def loads(tile, s):
        rows = pl.ds(pl.multiple_of(tile * FFN_TM, FFN_TM), FFN_TM)
        return (pltpu.make_async_copy(h1_hbm.at[rows, :], acc.at[s], load_sem.at[s, 0]),
                pltpu.make_async_copy(hn_hbm.at[rows, :], act.at[s], load_sem.at[s, 1]))

    def stores(tile, s):
        rows = pl.ds(pl.multiple_of(tile * FFN_TM, FFN_TM), FFN_TM)
        if last:
            return (pltpu.make_async_copy(acc.at[s], out_hbm.at[rows, :], store_sem.at[s, 0]),)
        return (pltpu.make_async_copy(acc.at[s], h2_hbm.at[rows, :], store_sem.at[s, 0]),
                pltpu.make_async_copy(act.at[s], xn_hbm.at[rows, :], store_sem.at[s, 1]))

    @pl.when((i == 0) & (f == 0))
    def _():
        for c in loads(0, 0):
            c.start()

    @pl.when(f == 0)
    def _():
        for c in loads(i, slot):
            c.wait()

    @pl.when((f == 1) & (i >= 1))
    def _():
        for c in stores(i - 1, 1 - slot):
            c.wait()

    @pl.when((f == 1) & (i + 1 < n_rows))
    def _():
        for c in loads(i + 1, 1 - slot):
            c.start()

    acc_s, act_s = acc.at[slot], act.at[slot]
    hn = act_s[...]
    acts = []
    for si in range(FFN_TF // FFN_SUB):
        sl = slice(si * FFN_SUB, (si + 1) * FFN_SUB)
        gate = jnp.dot(hn, wg_ref[:, sl], preferred_element_type=F32)
        up = jnp.dot(hn, wu_ref[:, sl], preferred_element_type=F32)
        acts.append((gate * up * (1.0 / (1.0 + jnp.exp2(gate * -LOG2E)))).astype(BF16))
    a = jnp.concatenate(acts, axis=1)
    for ci in range(D_MODEL // FFN_DOWN_TN):
        cols = slice(ci * FFN_DOWN_TN, (ci + 1) * FFN_DOWN_TN)
        acc_s[:, cols] += jnp.dot(a, wd_ref[:, cols], preferred_element_type=F32)

    @pl.when(f == nf - 1)
    def _():
        y = _rmsnorm_rows(acc_s[...], gn_ref[...])
        if last:
            acc_s[...] = y
        else:
            act_s[...] = y.astype(BF16)
        for c in stores(i, slot):
            c.start()

    @pl.when((f == nf - 1) & (i == n_rows - 1))
    def _():
        for c in stores(i, slot):
            c.wait()


def _ffn(hn, h1, w_gu, w_down, g_next, last):
    T = h1.shape[0]
    nf = D_FF // FFN_TF
    hbm = pl.BlockSpec(memory_space=pl.ANY)
    rows_f32 = jax.ShapeDtypeStruct((T, D_MODEL), F32)
    return pl.pallas_call(
        functools.partial(_ffn_kernel, last), grid=(T // FFN_TM, nf),
        in_specs=[hbm, hbm,
                  pl.BlockSpec((None, D_MODEL, FFN_TF), lambda i, f: (0, 0, f)),
                  pl.BlockSpec((None, D_MODEL, FFN_TF), lambda i, f: (0, 0, nf + f)),
                  pl.BlockSpec((None, FFN_TF, D_MODEL), lambda i, f: (0, f, 0)),
                  pl.BlockSpec((1, D_MODEL), lambda i, f: (0, 0))],
        out_specs=hbm if last else [hbm, hbm],
        out_shape=rows_f32 if last else [rows_f32, jax.ShapeDtypeStruct((T, D_MODEL), BF16)],
        scratch_shapes=[pltpu.VMEM((2, FFN_TM, D_MODEL), F32), pltpu.VMEM((2, FFN_TM, D_MODEL), BF16),
                        pltpu.SemaphoreType.DMA((2, 2)), pltpu.SemaphoreType.DMA((2, 2))],
        compiler_params=_params(2), name="swiglu_ffn")(hn, h1, w_gu, w_gu, w_down, g_next)


def kernel(x, positions, norm1_g, w_in, b_in, sinks, sgu_ln_g, sgu_ln_b, sgu_w, sgu_b,
           w_attn_branch, w_sgu_branch, w_out, norm2_g, w_gate_up, w_down, final_g):
    B, S, D = x.shape
    T = B * S
    depth = w_in.shape[0]
    assert D == D_MODEL and S % MIX_TM == 0 and T % FFN_TM == 0 and w_in.shape[-1] == IN_WIDTH

    inv_freq = ROPE_THETA ** (-jnp.arange(0, ROPE_DIM, 2, dtype=F32) / ROPE_DIM)
    lane = jnp.arange(LANES) % HEAD_DIM
    invf_lanes = jnp.where(lane < ROPE_DIM, inv_freq[lane % ROPE_HALF], 0.0)[None, :]

    sgu_w_b, sgu_b_c = sgu_w.astype(BF16), sgu_b[..., None]

    h = x.reshape(T, D)
    xn, *tabs, w_qkv, w_z, w_g = _rmsnorm(h, norm1_g[0][None, :], positions.reshape(T, 1), invf_lanes, w_in)
    for l in range(depth):
        b_qkv, b_z, b_g = (b_in[l, lo:hi][None, None, :] for lo, hi in IN_SEGMENTS)
        gates, w_gu, w_dn = _proj(_gates_kernel, "proj_gates", xn, w_g, b_g, tn=2048,
                                  cast=((w_gate_up, w_down), l))
        z, w_ab, w_sb, w_o = _proj(_gelu_kernel, "proj_gelu", xn, w_z, b_z, tn=2048,
                                   cast=((w_attn_branch, w_sgu_branch, w_out), l))
        q, kv = _qkv_proj(xn, w_qkv, b_qkv, tabs)
        mix = functools.partial(_mixer, q, kv, z, gates, h, sinks[l], sgu_ln_g[l][None, :],
                                sgu_ln_b[l][None, :], sgu_w_b[l], sgu_b_c[l], w_ab, w_sb, w_o,
                                norm2_g[l][None, None, :], S)
        if l == depth - 1:
            h1, hn = mix()
            out = _ffn(hn, h1, w_gu, w_dn, final_g[None, :], last=True)
        else:
            h1, hn, w_qkv, w_z, w_g = mix(cast_in=(w_in, l + 1))
            h, xn = _ffn(hn, h1, w_gu, w_dn, norm1_g[l + 1][None, :], last=False)
    return out.reshape(B, S, D)
```

```python
import functools

import jax
import jax.numpy as jnp
from jax import lax
from jax.experimental import pallas as pl
from jax.experimental.pallas import tpu as pltpu

D_MODEL = 2048
N_Q_HEADS = 16
N_KV_HEADS = 4
HEAD_DIM = 64
Q_PER_KV = N_Q_HEADS // N_KV_HEADS
ATTN_WIDTH = N_Q_HEADS * HEAD_DIM
KV_WIDTH = N_KV_HEADS * HEAD_DIM
WINDOW = 128
ROPE_THETA = 500000.0
ROPE_DIM = HEAD_DIM // 4
ROPE_HALF = ROPE_DIM // 2
SGU_WIDTH = D_MODEL // 2
SGU_GROUPS = 8
SGU_GROUP_DIM = SGU_WIDTH // SGU_GROUPS
CHUNK = 128
D_FF = 5632
EPS = 1e-5
NEG = -1e30
LOG2E = 1.4426950408889634

OFF_Q = 0
OFF_K = OFF_Q + ATTN_WIDTH
OFF_Z = OFF_K + 2 * KV_WIDTH
OFF_G = OFF_Z + 2 * SGU_WIDTH
IN_WIDTH = OFF_G + 2 * D_MODEL
IN_SEGMENTS = ((OFF_Q, OFF_Z), (OFF_Z, OFF_G), (OFF_G, IN_WIDTH))
KVDUP_WIDTH = 2 * KV_WIDTH

LANES = 128
VMEM_LIMIT = 52 * 1024 * 1024

BF16 = jnp.bfloat16
F32 = jnp.float32


def _params(n_axes):
    return pltpu.CompilerParams(dimension_semantics=("arbitrary",) * n_axes,
                                vmem_limit_bytes=VMEM_LIMIT)


def _rmsnorm_rows(x, g):
    return x * lax.rsqrt(jnp.mean(x * x, axis=-1, keepdims=True) + EPS) * g


def _low_half_lanes(shape=(1, LANES)):
    return lax.broadcasted_iota(jnp.int32, shape, len(shape) - 1) % LANES < HEAD_DIM


def _rope_tables(pos_ref, invf_ref, c_ref, s1_ref, s2_ref):
    lane = lax.broadcasted_iota(jnp.int32, c_ref.shape, 1) % HEAD_DIM
    ang = pos_ref[...].astype(F32) * invf_ref[...]
    cos, sin = jnp.cos(ang), jnp.sin(ang)
    first, second = lane < ROPE_HALF, (lane >= ROPE_HALF) & (lane < ROPE_DIM)
    c_ref[...] = jnp.where(lane < ROPE_DIM, cos, 1.0)
    s1_ref[...] = jnp.where(first, -sin, 0.0)
    s2_ref[...] = jnp.where(second, sin, 0.0)


def _cast_in_segments(w32_ref, seg_refs):
    for (lo, hi), seg_ref in zip(IN_SEGMENTS, seg_refs):
        seg_ref[...] = w32_ref[:, lo:hi].astype(BF16)


def _in_segment_specs(w_in32, layer, n_steps):
    rows = D_MODEL // n_steps
    in_spec = pl.BlockSpec((None, rows, IN_WIDTH), lambda i: (layer, i, 0))
    out_specs = [pl.BlockSpec((None, rows, hi - lo), lambda i: (0, i, 0)) for lo, hi in IN_SEGMENTS]
    out_shape = [jax.ShapeDtypeStruct((1, D_MODEL, hi - lo), BF16) for lo, hi in IN_SEGMENTS]
    return in_spec, out_specs, out_shape


def _norm_kernel(x_ref, g_ref, pos_ref, invf_ref, w32_ref, o_ref, c_ref, s1_ref, s2_ref, *seg_refs):
    _rope_tables(pos_ref, invf_ref, c_ref, s1_ref, s2_ref)
    _cast_in_segments(w32_ref, seg_refs)
    o_ref[...] = _rmsnorm_rows(x_ref[...], g_ref[...]).astype(o_ref.dtype)


def _rmsnorm(x, g, pos, invf_lanes, w_in32, tm=1024):
    T, D = x.shape
    w_spec, seg_specs, seg_shapes = _in_segment_specs(w_in32, 0, T // tm)
    tab_spec = pl.BlockSpec((tm, LANES), lambda i: (i, 0))
    tab_shape = jax.ShapeDtypeStruct((T, LANES), F32)
    return pl.pallas_call(
        _norm_kernel, grid=(T // tm,),
        in_specs=[pl.BlockSpec((tm, D), lambda i: (i, 0)), pl.BlockSpec((1, D), lambda i: (0, 0)),
                  pl.BlockSpec((tm, 1), lambda i: (i, 0)), pl.BlockSpec((1, LANES), lambda i: (0, 0)),
                  w_spec],
        out_specs=[pl.BlockSpec((tm, D), lambda i: (i, 0))] + [tab_spec] * 3 + seg_specs,
        out_shape=[jax.ShapeDtypeStruct((T, D), BF16)] + [tab_shape] * 3 + seg_shapes,
        compiler_params=_params(1), name="rmsnorm0")(x, g, pos, invf_lanes, w_in32)


PROJ_TM = 1024
PROJ_SUB_M = 128
PROJ_SUB = 256


def _rope(r, c, s1, s2):
    return (r * c + pltpu.roll(r, LANES - ROPE_HALF, 1) * s1
            + pltpu.roll(r, ROPE_HALF, 1) * s2)


def _sub_tiles(xn_ref, w_ref, b_ref):
    for mi in range(xn_ref.shape[0] // PROJ_SUB_M):
        rows = slice(mi * PROJ_SUB_M, (mi + 1) * PROJ_SUB_M)
        xn = xn_ref[rows, :]
        for si in range(w_ref.shape[1] // PROJ_SUB):
            sl = slice(si * PROJ_SUB, (si + 1) * PROJ_SUB)
            yield rows, si, jnp.dot(xn, w_ref[:, sl], preferred_element_type=F32) + b_ref[:, sl]


def _cast_blocks(rest):
    n = len(rest) // 2
    for src, dst in zip(rest[:n], rest[n + 1:]):
        dst[...] = src[...].astype(BF16)
    return rest[n]


def _gates_kernel(xn_ref, w_ref, b_ref, *rest):
    o_ref = _cast_blocks(rest)
    for rows, si, r in _sub_tiles(xn_ref, w_ref, b_ref):
        o_ref[rows, si * PROJ_SUB:(si + 1) * PROJ_SUB] = (
            1.0 / (1.0 + jnp.exp2(r * -LOG2E))).astype(o_ref.dtype)


def _gelu_kernel(xn_ref, w_ref, b_ref, *rest):
    o_ref = _cast_blocks(rest)
    for rows, si, r in _sub_tiles(xn_ref, w_ref, b_ref):
        o_ref[rows, si * PROJ_SUB:(si + 1) * PROJ_SUB] = (
            0.5 * r * (1.0 + lax.erf(r * (2.0 ** -0.5)))).astype(o_ref.dtype)


def _qkv_kernel(xn_ref, w_ref, b_ref, c_ref, s1_ref, s2_ref, q_ref, kv_ref):
    low = _low_half_lanes()
    for rows, si, r in _sub_tiles(xn_ref, w_ref, b_ref):
        c, s1, s2 = c_ref[rows, :], s1_ref[rows, :], s2_ref[rows, :]
        for gi in range(PROJ_SUB // LANES):
            x = r[:, gi * LANES:(gi + 1) * LANES]
            col = si * PROJ_SUB + gi * LANES
            if col < ATTN_WIDTH:
                xq = _rope(x, c, s1, s2) * (HEAD_DIM ** -0.5 * LOG2E)
                q_ref[rows, 2 * col:2 * col + LANES] = jnp.where(low, xq, 0.0).astype(q_ref.dtype)
                q_ref[rows, 2 * col + LANES:2 * col + 2 * LANES] = jnp.where(low, 0.0, xq).astype(q_ref.dtype)
                continue
            if col < ATTN_WIDTH + KV_WIDTH:
                x = _rope(x, c, s1, s2)
            swapped = pltpu.roll(x, HEAD_DIM, 1)
            lo = 2 * (col - ATTN_WIDTH)
            kv_ref[rows, lo:lo + LANES] = jnp.where(low, x, swapped).astype(kv_ref.dtype)
            kv_ref[rows, lo + LANES:lo + 2 * LANES] = jnp.where(low, swapped, x).astype(kv_ref.dtype)


def _qkv_proj(xn, w, b, tabs):
    T, tm = xn.shape[0], PROJ_TM
    width = w.shape[-1]
    tab_spec = pl.BlockSpec((tm, LANES), lambda i: (i, 0))
    return pl.pallas_call(
        _qkv_kernel, grid=(T // tm,),
        in_specs=[pl.BlockSpec((tm, D_MODEL), lambda i: (i, 0)),
                  pl.BlockSpec((None, D_MODEL, width), lambda i: (0, 0, 0)),
                  pl.BlockSpec((None, 1, width), lambda i: (0, 0, 0))] + [tab_spec] * 3,
        out_specs=[pl.BlockSpec((tm, 2 * ATTN_WIDTH), lambda i: (i, 0)),
                   pl.BlockSpec((tm, 2 * KVDUP_WIDTH), lambda i: (i, 0))],
        out_shape=[jax.ShapeDtypeStruct((T, 2 * ATTN_WIDTH), BF16),
                   jax.ShapeDtypeStruct((T, 2 * KVDUP_WIDTH), BF16)],
        compiler_params=_params(1), name="proj_qkv")(xn, w, b, *tabs)


def _proj(body, name, xn, w, b, tn, cast):
    T, tm = xn.shape[0], PROJ_TM
    width = w.shape[-1]
    n_i, n_j = T // tm, width // tn
    in_specs = [pl.BlockSpec((tm, D_MODEL), lambda i, j: (i, 0)),
                pl.BlockSpec((None, D_MODEL, tn), lambda i, j: (0, 0, j)),
                pl.BlockSpec((None, 1, tn), lambda i, j: (0, 0, j))]
    out_specs = [pl.BlockSpec((tm, tn), lambda i, j: (i, j))]
    out_shape = [jax.ShapeDtypeStruct((T, width), BF16)]
    args = [xn, w, b]
    arrays, layer = cast
    for a in arrays:
        blk = (None, a.shape[1] // n_i, a.shape[2] // n_j)
        in_specs.append(pl.BlockSpec(blk, lambda i, j: (layer, i, j)))
        out_specs.append(pl.BlockSpec(blk, lambda i, j: (0, i, j)))
        out_shape.append(jax.ShapeDtypeStruct((1,) + a.shape[1:], BF16))
        args.append(a)
    return pl.pallas_call(
        body, grid=(n_i, n_j), in_specs=in_specs, out_specs=out_specs, out_shape=out_shape,
        compiler_params=_params(2), name=name)(*args)


def _attention_rows(seq_start, sinks_ref, q_ref, kc_ref, vc_ref, kp_ref, vp_ref, o_ref):
    qi = lax.broadcasted_iota(jnp.int32, (WINDOW, 2 * WINDOW), 0)
    kj = lax.broadcasted_iota(jnp.int32, (WINDOW, 2 * WINDOW), 1)
    rel = qi + WINDOW - kj
    band_ok = (rel >= 0) & (rel < WINDOW)
    low = _low_half_lanes()
    ones_low = jnp.broadcast_to(jnp.where(low, 1.0, 0.0).astype(BF16), (2 * WINDOW, LANES))
    ones_high = jnp.broadcast_to(jnp.where(low, 0.0, 1.0).astype(BF16), (2 * WINDOW, LANES))
    for c in range(q_ref.shape[0] // WINDOW):
        rows = slice(c * WINDOW, (c + 1) * WINDOW)
        if c == 0:
            first_key = jnp.where(seq_start, WINDOW, 0)
            mask = band_ok & (kj >= first_key)
            k_prev, v_prev = kp_ref[...], vp_ref[...]
        else:
            mask = band_ok
            prev_rows = slice((c - 1) * WINDOW, c * WINDOW)
            k_prev, v_prev = kc_ref[prev_rows, :], vc_ref[prev_rows, :]
        k_band = jnp.concatenate([k_prev, kc_ref[rows, :]], axis=0)
        v_band = jnp.concatenate([v_prev, vc_ref[rows, :]], axis=0)
        for g in range(N_KV_HEADS):
            lanes_g = slice(g * LANES, (g + 1) * LANES)
            kg, vg = k_band[:, lanes_g], v_band[:, lanes_g]
            zero = jnp.zeros_like(vg)
            rhs_pv = jnp.concatenate(
                [jnp.concatenate([jnp.where(low, vg, zero), ones_low], axis=1),
                 jnp.concatenate([jnp.where(low, zero, vg), ones_high], axis=1)], axis=0)
            lhs = jnp.concatenate(
                [q_ref[rows, (Q_PER_KV * g + i) * LANES:(Q_PER_KV * g + i + 1) * LANES]
                 for i in range(Q_PER_KV)], axis=0)
            s = lax.dot_general(lhs, kg, (((1,), (1,)), ((), ())), preferred_element_type=F32)
            probs, sink_terms = [], []
            for i in range(Q_PER_KV):
                sink = sinks_ref[g * Q_PER_KV + i] * LOG2E
                si = jnp.where(mask, s[i * WINDOW:(i + 1) * WINDOW, :], NEG)
                m = jnp.maximum(jnp.max(si, axis=-1, keepdims=True), sink)
                probs.append(jnp.exp2(si - m).astype(BF16))
                sink_terms.append(jnp.exp2(sink - m))
            for pair in range(2):
                p2 = jnp.concatenate(probs[2 * pair:2 * pair + 2], axis=1)
                r = jnp.dot(p2, rhs_pv, preferred_element_type=F32)
                den = r[:, LANES:] + jnp.where(low, sink_terms[2 * pair], sink_terms[2 * pair + 1])
                out_lanes = slice((2 * g + pair) * LANES, (2 * g + pair + 1) * LANES)
                o_ref[rows, out_lanes] = (r[:, :LANES] / den).astype(o_ref.dtype)


def _sgu_norm(v_ref, lng_ref, lnb_ref):
    v = v_ref[...].astype(F32)
    mu = jnp.mean(v, axis=-1, keepdims=True)
    d = v - mu
    var = jnp.mean(d * d, axis=-1, keepdims=True)
    return (d * lax.rsqrt(var + EPS) * lng_ref[...] + lnb_ref[...]).astype(BF16)


def _sgu_group(g, vn, u_ref, w_ref, b_ref, o_ref):
    ti = lax.broadcasted_iota(jnp.int32, (CHUNK, CHUNK), 0)
    si = lax.broadcasted_iota(jnp.int32, (CHUNK, CHUNK), 1)
    cols = slice(g * SGU_GROUP_DIM, (g + 1) * SGU_GROUP_DIM)
    w = jnp.where(si <= ti, w_ref[g], jnp.zeros_like(w_ref[g]))
    bias = b_ref[g]
    for c in range(u_ref.shape[0] // CHUNK):
        rows = slice(c * CHUNK, (c + 1) * CHUNK)
        sv = jnp.dot(w, vn[rows, cols], preferred_element_type=F32) + bias
        o_ref[rows, cols] = (u_ref[rows, cols].astype(F32) * sv).astype(o_ref.dtype)


MIX_TM = 256
MRG_TN = 1024


def _mixer_kernel(blocks_per_seq, sinks_ref, q_ref, kc_ref, vc_ref, kp_ref, vp_ref, u_ref, v_ref,
                  lng_ref, lnb_ref, sw_ref, sb_ref, g_ref, h_ref, wa_ref, ws_ref, wo_ref, g2_ref, *rest):
    if len(rest) == 6:
        h1_ref, hn_ref, ya_ref, ys_ref, part_ref, merged_ref = rest
    else:
        next_in_ref, h1_ref, hn_ref, *seg_refs, ya_ref, ys_ref, part_ref, merged_ref = rest
        _cast_in_segments(next_in_ref, seg_refs)
    seq_start = (pl.program_id(0) * (MIX_TM // WINDOW)) % blocks_per_seq == 0
    _attention_rows(seq_start, sinks_ref, q_ref, kc_ref, vc_ref, kp_ref, vp_ref, ya_ref)
    ya = ya_ref[...]
    vn = _sgu_norm(v_ref, lng_ref, lnb_ref)
    groups_per_tile = SGU_GROUPS // (D_MODEL // MRG_TN)
    for j in range(D_MODEL // MRG_TN):
        cols = slice(j * MRG_TN, (j + 1) * MRG_TN)
        part_ref[:, cols] = g_ref[:, cols].astype(F32) * jnp.dot(
            ya, wa_ref[:, cols], preferred_element_type=F32)
        for g in range(j * groups_per_tile, (j + 1) * groups_per_tile):
            _sgu_group(g, vn, u_ref, sw_ref, sb_ref, ys_ref)
    ys = ys_ref[...]
    for j in range(D_MODEL // MRG_TN):
        cols = slice(j * MRG_TN, (j + 1) * MRG_TN)
        gs_cols = slice(D_MODEL + j * MRG_TN, D_MODEL + (j + 1) * MRG_TN)
        s = jnp.dot(ys, ws_ref[:, cols], preferred_element_type=F32)
        merged_ref[:, cols] = (g_ref[:, gs_cols].astype(F32) * s + part_ref[:, cols]).astype(BF16)
    merged = merged_ref[...]
    for j in range(D_MODEL // MRG_TN):
        cols = slice(j * MRG_TN, (j + 1) * MRG_TN)
        h1_ref[:, cols] = h_ref[:, cols] + jnp.dot(merged, wo_ref[:, cols], preferred_element_type=F32)
    hn_ref[...] = _rmsnorm_rows(h1_ref[...], g2_ref[...]).astype(hn_ref.dtype)


def _mixer(q, kv, z, gates, h, sinks, ln_g, ln_b, sgu_w, sgu_b, w_ab, w_sb, w_out, g2, seq_len,
           cast_in=None):
    T, tm = h.shape[0], MIX_TM
    n_rows = T // tm
    row = lambda width, col=0: pl.BlockSpec((tm, width), lambda i: (i, col))
    prev = lambda col: pl.BlockSpec((WINDOW, KVDUP_WIDTH),
                                    lambda i: (jnp.maximum(i * (tm // WINDOW) - 1, 0), col))
    const = lambda shape: pl.BlockSpec(shape, lambda i: (0,) * len(shape))
    resident = lambda shape: pl.BlockSpec((None,) + shape, lambda i: (0, 0, 0),
                                          pipeline_mode=pl.Buffered(1))
    in_specs = [pl.BlockSpec(memory_space=pltpu.SMEM),
                row(2 * ATTN_WIDTH), row(KVDUP_WIDTH, 0), row(KVDUP_WIDTH, 1), prev(0), prev(1),
                row(SGU_WIDTH, 0), row(SGU_WIDTH, 1), const((1, SGU_WIDTH)), const((1, SGU_WIDTH)),
                const((SGU_GROUPS, CHUNK, CHUNK)), const((SGU_GROUPS, CHUNK, 1)),
                row(2 * D_MODEL), row(D_MODEL),
                resident((ATTN_WIDTH, D_MODEL)), resident((SGU_WIDTH, D_MODEL)),
                resident((D_MODEL, D_MODEL)), resident((1, D_MODEL))]
    out_specs = [row(D_MODEL), row(D_MODEL)]
    out_shape = [jax.ShapeDtypeStruct((T, D_MODEL), F32), jax.ShapeDtypeStruct((T, D_MODEL), BF16)]
    args = [sinks, q, kv, kv, kv, kv, z, z, ln_g, ln_b, sgu_w, sgu_b, gates, h, w_ab, w_sb, w_out, g2]
    if cast_in is not None:
        w_spec, seg_specs, seg_shapes = _in_segment_specs(*cast_in, n_rows)
        in_specs.append(w_spec)
        args.append(cast_in[0])
        out_specs += seg_specs
        out_shape += seg_shapes
    return pl.pallas_call(
        functools.partial(_mixer_kernel, seq_len // WINDOW), grid=(n_rows,),
        in_specs=in_specs, out_specs=out_specs, out_shape=out_shape,
        scratch_shapes=[pltpu.VMEM((tm, ATTN_WIDTH), BF16), pltpu.VMEM((tm, SGU_WIDTH), BF16),
                        pltpu.VMEM((tm, D_MODEL), F32), pltpu.VMEM((tm, D_MODEL), BF16)],
        compiler_params=_params(1), name="token_mixer")(*args)


FFN_TM = 1024
FFN_TF = 512
FFN_SUB = 256
FFN_DOWN_TN = 512


def _ffn_kernel(last, hn_hbm, h1_hbm, wg_ref, wu_ref, wd_ref, gn_ref, *rest):
    if last:
        out_hbm, acc, act, load_sem, store_sem = rest
    else:
        h2_hbm, xn_hbm, acc, act, load_sem, store_sem = rest
    i, f = pl.program_id(0), pl.program_id(1)
    n_rows, nf = pl.num_programs(0), pl.num_programs(1)
    slot = i % 2

    def loads(tile, s):
        rows = pl.ds(pl.multiple_of(tile * FFN_TM, FFN_TM), FFN_TM)
        return (pltpu.make_async_copy(h1_hbm.at[rows, :], acc.at[s], load_sem.at[s, 0]),
                pltpu.make_async_copy(hn_hbm.at[rows, :], act.at[s], load_sem.at[s, 1]))

    def stores(tile, s):
        rows = pl.ds(pl.multiple_of(tile * FFN_TM, FFN_TM), FFN_TM)
        if last:
            return (pltpu.make_async_copy(acc.at[s], out_hbm.at[rows, :], store_sem.at[s, 0]),)
        return (pltpu.make_async_copy(acc.at[s], h2_hbm.at[rows, :], store_sem.at[s, 0]),
                pltpu.make_async_copy(act.at[s], xn_hbm.at[rows, :], store_sem.at[s, 1]))

    @pl.when((i == 0) & (f == 0))
    def _():
        for c in loads(0, 0):
            c.start()

    @pl.when(f == 0)
    def _():
        for c in loads(i, slot):
            c.wait()

    @pl.when((f == 1) & (i >= 1))
    def _():
        for c in stores(i - 1, 1 - slot):
            c.wait()

    @pl.when((f == 1) & (i + 1 < n_rows))
    def _():
        for c in loads(i + 1, 1 - slot):
            c.start()

    acc_s, act_s = acc.at[slot], act.at[slot]
    hn = act_s[...]
    acts = []
    for si in range(FFN_TF // FFN_SUB):
        sl = slice(si * FFN_SUB, (si + 1) * FFN_SUB)
        gate = jnp.dot(hn, wg_ref[:, sl], preferred_element_type=F32)
        up = jnp.dot(hn, wu_ref[:, sl], preferred_element_type=F32)
        acts.append((gate * up * (1.0 / (1.0 + jnp.exp2(gate * -LOG2E)))).astype(BF16))
    a = jnp.concatenate(acts, axis=1)
    for ci in range(D_MODEL // FFN_DOWN_TN):
        cols = slice(ci * FFN_DOWN_TN, (ci + 1) * FFN_DOWN_TN)
        acc_s[:, cols] += jnp.dot(a, wd_ref[:, cols], preferred_element_type=F32)

    @pl.when(f == nf - 1)
    def _():
        y = _rmsnorm_rows(acc_s[...], gn_ref[...])
        if last:
            acc_s[...] = y
        else:
            act_s[...] = y.astype(BF16)
        for c in stores(i, slot):
            c.start()

    @pl.when((f == nf - 1) & (i == n_rows - 1))
    def _():
        for c in stores(i, slot):
            c.wait()


def _ffn(hn, h1, w_gu, w_down, g_next, last):
    T = h1.shape[0]
    nf = D_FF // FFN_TF
    hbm = pl.BlockSpec(memory_space=pl.ANY)
    rows_f32 = jax.ShapeDtypeStruct((T, D_MODEL), F32)
    return pl.pallas_call(
        functools.partial(_ffn_kernel, last), grid=(T // FFN_TM, nf),
        in_specs=[hbm, hbm,
                  pl.BlockSpec((None, D_MODEL, FFN_TF), lambda i, f: (0, 0, f)),
                  pl.BlockSpec((None, D_MODEL, FFN_TF), lambda i, f: (0, 0, nf + f)),
                  pl.BlockSpec((None, FFN_TF, D_MODEL), lambda i, f: (0, f, 0)),
                  pl.BlockSpec((1, D_MODEL), lambda i, f: (0, 0))],
        out_specs=hbm if last else [hbm, hbm],
        out_shape=rows_f32 if last else [rows_f32, jax.ShapeDtypeStruct((T, D_MODEL), BF16)],
        scratch_shapes=[pltpu.VMEM((2, FFN_TM, D_MODEL), F32), pltpu.VMEM((2, FFN_TM, D_MODEL), BF16),
                        pltpu.SemaphoreType.DMA((2, 2)), pltpu.SemaphoreType.DMA((2, 2))],
        compiler_params=_params(2), name="swiglu_ffn")(hn, h1, w_gu, w_gu, w_down, g_next)


def kernel(x, positions, norm1_g, w_in, b_in, sinks, sgu_ln_g, sgu_ln_b, sgu_w, sgu_b,
           w_attn_branch, w_sgu_branch, w_out, norm2_g, w_gate_up, w_down, final_g):
    B, S, D = x.shape
    T = B * S
    depth = w_in.shape[0]
    assert D == D_MODEL and S % MIX_TM == 0 and T % FFN_TM == 0 and w_in.shape[-1] == IN_WIDTH

    inv_freq = ROPE_THETA ** (-jnp.arange(0, ROPE_DIM, 2, dtype=F32) / ROPE_DIM)
    lane = jnp.arange(LANES) % HEAD_DIM
    invf_lanes = jnp.where(lane < ROPE_DIM, inv_freq[lane % ROPE_HALF], 0.0)[None, :]

    sgu_w_b, sgu_b_c = sgu_w.astype(BF16), sgu_b[..., None]

    h = x.reshape(T, D)
    xn, *tabs, w_qkv, w_z, w_g = _rmsnorm(h, norm1_g[0][None, :], positions.reshape(T, 1), invf_lanes, w_in)
    for l in range(depth):
        b_qkv, b_z, b_g = (b_in[l, lo:hi][None, None, :] for lo, hi in IN_SEGMENTS)
        gates, w_gu, w_dn = _proj(_gates_kernel, "proj_gates", xn, w_g, b_g, tn=2048,
                                  cast=((w_gate_up, w_down), l))
        z, w_ab, w_sb, w_o = _proj(_gelu_kernel, "proj_gelu", xn, w_z, b_z, tn=2048,
                                   cast=((w_attn_branch, w_sgu_branch, w_out), l))
        q, kv = _qkv_proj(xn, w_qkv, b_qkv, tabs)
        mix = functools.partial(_mixer, q, kv, z, gates, h, sinks[l], sgu_ln_g[l][None, :],
                                sgu_ln_b[l][None, :], sgu_w_b[l], sgu_b_c[l], w_ab, w_sb, w_o,
                                norm2_g[l][None, None, :], S)
        if l == depth - 1:
            h1, hn = mix()
            out = _ffn(hn, h1, w_gu, w_dn, final_g[None, :], last=True)
        else:
            h1, hn, w_qkv, w_z, w_g = mix(cast_in=(w_in, l + 1))
            h, xn = _ffn(hn, h1, w_gu, w_dn, norm1_g[l + 1][None, :], last=False)
    return out.reshape(B, S, D)
```
